```python
import jax
import jax.numpy as jnp
from jax import lax
import numpy as np

D_MODEL = 1024
BATCH = 16
SEQ = 2048
DEPTH = 1
DEC_BATCH = 16
DEC_SEQ = 32
PAST_LEN = 1024

CHUNK = 64
ROPE_THETA = 10000.0
EPS = 1e-6
N_HEADS_A = 8
N_KV_A = 2
HEAD_DIM_A = D_MODEL // 16
N_HEADS_IDX = 8
IDX_DIM = D_MODEL // 16
TOPK_MAX = 256
N_HEADS_G = 4
DK_G = D_MODEL // 16
DV_G = D_MODEL // 8
GATE_RANK = 16
GATE_TAU = 16.0
N_MEM = 256
N_HEADS_M = 4
HEAD_DIM_M = D_MODEL // 8
N_EXPERTS = 32
TOP_K = 4
D_EXPERT = D_MODEL
SWIGLU_LIMIT = 7.0
SWIGLU_ALPHA = 1.702
EXPERT_BLOCK = 256
N_BRANCH = 3

W_A = N_HEADS_A * HEAD_DIM_A
W_KVA = N_KV_A * HEAD_DIM_A
W_QI = N_HEADS_IDX * IDX_DIM
W_KG = N_HEADS_G * DK_G
W_VG = N_HEADS_G * DV_G
W_M = N_HEADS_M * HEAD_DIM_M
IN_WIDTHS = (W_A, W_KVA, W_KVA, W_QI, IDX_DIM, N_HEADS_IDX, W_KG, W_KG, W_VG, GATE_RANK, W_VG, W_M, N_BRANCH * D_MODEL)
IN_WIDTH = sum(IN_WIDTHS)

kernel_name = 'hybrid_dsa_gla_mem_moe_stream_step'


def rms_norm(x, g):
    xf = x.astype(jnp.float32)
    y = xf * lax.rsqrt(jnp.mean(xf * xf, axis=-1, keepdims=True) + EPS)
    return (y * g.astype(jnp.float32)).astype(x.dtype)


def rope(x, pos):
    half = x.shape[-1] // 2
    inv_freq = ROPE_THETA ** (-jnp.arange(half, dtype=jnp.float32) / half)
    ang = pos.astype(jnp.float32)[:, None] * inv_freq[None, :]
    cos = jnp.cos(ang)[None, :, None, :]
    sin = jnp.sin(ang)[None, :, None, :]
    xf = x.astype(jnp.float32)
    x1, x2 = xf[..., :half], xf[..., half:]
    return jnp.concatenate([x1 * cos - x2 * sin, x2 * cos + x1 * sin], axis=-1).astype(x.dtype)


def to_blocks(a):
    return jnp.moveaxis(a.reshape(a.shape[0], a.shape[1] // CHUNK, CHUNK, *a.shape[2:]), 1, 0)


def in_projection(h, pos, w_in, b_gate, w_gla_rank, b_gla_rank):
    B, T, _ = h.shape
    z = h @ w_in
    cuts, acc = [], 0
    for w in IN_WIDTHS[:-1]:
        acc += w
        cuts.append(acc)
    qa, ka, va, qi, ki, wi, qg, kg, vg, ag, og, qm, gt = jnp.split(z, cuts, axis=-1)
    qa = rope(qa.reshape(B, T, N_HEADS_A, HEAD_DIM_A), pos)
    ka = rope(ka.reshape(B, T, N_KV_A, HEAD_DIM_A), pos)
    va = va.reshape(B, T, N_KV_A, HEAD_DIM_A)
    qi = rope(qi.reshape(B, T, N_HEADS_IDX, IDX_DIM), pos)
    ki = rope(ki.reshape(B, T, 1, IDX_DIM), pos)[:, :, 0]
    wi = wi * (N_HEADS_IDX ** -0.5 * IDX_DIM ** -0.5)
    qg = qg.reshape(B, T, N_HEADS_G, DK_G) * DK_G ** -0.5
    kg = kg.reshape(B, T, N_HEADS_G, DK_G)
    vg = vg.reshape(B, T, N_HEADS_G, DV_G)
    lg = (jax.nn.log_sigmoid(ag @ w_gla_rank + b_gla_rank) / GATE_TAU).reshape(B, T, N_HEADS_G, DK_G)
    qm = qm.reshape(B, T, N_HEADS_M, HEAD_DIM_M)
    gates = jax.nn.sigmoid(gt + b_gate)
    return qa, ka, va, qi, ki, wi, qg, kg, vg, lg, og, qm, gates


def dsa_attend(q, qi, wi, q_pos, k, v, ki, k_pos, topk):
    B, Tq = q.shape[:2]
    dots = jnp.einsum('bthd,bsd->btsh', qi, ki).astype(jnp.float32)
    score = jnp.einsum('btsh,bth->bts', jax.nn.relu(dots), wi.astype(jnp.float32))
    admissible = (k_pos[None, :] // CHUNK) <= (q_pos[:, None] // CHUNK)
    score = jnp.where(admissible[None], score, -jnp.inf)
    top_score, idx = lax.top_k(score, topk)
    valid = jnp.isfinite(top_score)
    gather = jax.vmap(lambda a, i: a[i])
    k_sel = gather(k, idx)
    v_sel = gather(v, idx)
    qg = q.reshape(B, Tq, N_KV_A, N_HEADS_A // N_KV_A, HEAD_DIM_A)
    logits = jnp.einsum('btngd,btknd->btngk', qg, k_sel).astype(jnp.float32) * HEAD_DIM_A ** -0.5
    logits = jnp.where(valid[:, :, None, None, :], logits, -jnp.inf)
    p = jax.nn.softmax(logits, axis=-1).astype(v.dtype)
    o = jnp.einsum('btngk,btknd->btngd', p, v_sel)
    return o.reshape(B, Tq, W_A)


def dsa_prompt(qa, ka, va, qi, ki, wi, pos, topk):
    def one_block(blk):
        q_b, qi_b, wi_b, pos_b = blk
        return dsa_attend(q_b, qi_b, wi_b, pos_b, ka, va, ki, pos, topk)
    o = lax.map(one_block, (to_blocks(qa), to_blocks(qi), to_blocks(wi), pos.reshape(-1, CHUNK)))
    o = jnp.moveaxis(o, 0, 1)
    return o.reshape(o.shape[0], -1, W_A)


def gla_chunk(S, inp):
    q, k, v, lg = inp
    C = q.shape[1]
    b = jnp.cumsum(lg, axis=1)
    o_inter = jnp.einsum('bthk,bhkv->bthv', q * jnp.exp(b), S)
    causal = jnp.tril(jnp.ones((C, C), dtype=bool))
    diff = b[:, :, None] - b[:, None, :]
    decay = jnp.exp(jnp.where(causal[None, :, :, None, None], diff, -jnp.inf))
    A = jnp.einsum('bthk,bshk,btshk->bths', q, k, decay)
    o = o_inter + jnp.einsum('bths,bshv->bthv', A, v)
    b_last = b[:, -1]
    k_dec = k * jnp.exp(b_last[:, None] - b)
    S_new = jnp.exp(b_last)[..., None] * S + jnp.einsum('bshk,bshv->bhkv', k_dec, v)
    return S_new, o


def gla_prompt(q, k, v, lg):
    B = q.shape[0]
    S0 = jnp.zeros((B, N_HEADS_G, DK_G, DV_G), q.dtype)
    S, o = lax.scan(gla_chunk, S0, (to_blocks(q), to_blocks(k), to_blocks(v), to_blocks(lg)))
    o = jnp.moveaxis(o, 0, 1)
    return S, o.reshape(B, -1, N_HEADS_G, DV_G)


def gla_output(o, og, g_gla_norm):
    B, T = o.shape[:2]
    return rms_norm(o, g_gla_norm.reshape(N_HEADS_G, DV_G)).reshape(B, T, W_VG) * jax.nn.silu(og)


def memory_kv(mem, g_mem_norm, w_mem_kv):
    B = mem.shape[0]
    kv = rms_norm(mem, g_mem_norm) @ w_mem_kv
    mk, mv = jnp.split(kv, 2, axis=-1)
    return mk.reshape(B, N_MEM, N_HEADS_M, HEAD_DIM_M), mv.reshape(B, N_MEM, N_HEADS_M, HEAD_DIM_M)


def mem_attend(q, mk, mv):
    B, T = q.shape[:2]
    logits = jnp.einsum('bthd,bmhd->bhtm', q, mk).astype(jnp.float32) * HEAD_DIM_M ** -0.5
    p = jax.nn.softmax(logits, axis=-1).astype(mv.dtype)
    return jnp.einsum('bhtm,bmhd->bthd', p, mv).reshape(B, T, W_M)


def merge_branches(x, o_a, o_b, o_m, gates, w_br_a, w_br_b, w_br_m, w_o):
    g_a, g_b, g_m = jnp.split(gates, N_BRANCH, axis=-1)
    m = g_a * (o_a @ w_br_a) + g_b * (o_b @ w_br_b) + g_m * (o_m @ w_br_m)
    return x + m @ w_o


def moe_ffn(h, w_router, b_router, w_gu, b_gu, w_dn, b_dn):
    lead = h.shape[:-1]
    xt = h.reshape(-1, D_MODEL)
    T = xt.shape[0]
    logits = (xt @ w_router + b_router).astype(jnp.float32)
    top_val, top_idx = lax.top_k(logits, TOP_K)
    gate = jax.nn.softmax(top_val, axis=-1).astype(h.dtype)
    TK = T * TOP_K
    flat_e = top_idx.reshape(TK)
    flat_tok = jnp.arange(TK, dtype=jnp.int32) // TOP_K
    order = jnp.argsort(flat_e)
    sorted_e = flat_e[order]
    counts = jnp.bincount(flat_e, length=N_EXPERTS)
    padded = (counts + EXPERT_BLOCK - 1) // EXPERT_BLOCK * EXPERT_BLOCK
    starts = jnp.cumsum(counts) - counts
    pad_ends = jnp.cumsum(padded)
    pad_starts = pad_ends - padded
    dest = pad_starts[sorted_e] + jnp.arange(TK, dtype=jnp.int32) - starts[sorted_e]
    n_blocks = (TK + N_EXPERTS * (EXPERT_BLOCK - 1) + EXPERT_BLOCK - 1) // EXPERT_BLOCK
    n_slots = n_blocks * EXPERT_BLOCK
    slot_tok = jnp.full((n_slots,), T, dtype=jnp.int32).at[dest].set(flat_tok[order])
    slot_gate = jnp.zeros((n_slots,), h.dtype).at[dest].set(gate.reshape(TK)[order])
    block_e = jnp.minimum(jnp.searchsorted(pad_ends, jnp.arange(n_blocks, dtype=jnp.int32) * EXPERT_BLOCK, side='right'), N_EXPERTS - 1)
    x_pad = jnp.concatenate([xt, jnp.zeros((1, D_MODEL), xt.dtype)], axis=0)
    xb = x_pad[slot_tok].reshape(n_blocks, EXPERT_BLOCK, D_MODEL)

    def expert_block(args):
        xe, e = args
        gu = xe @ w_gu[e] + b_gu[e]
        g, u = jnp.split(gu, 2, axis=-1)
        g = jnp.minimum(g, SWIGLU_LIMIT)
        u = jnp.clip(u, -SWIGLU_LIMIT, SWIGLU_LIMIT)
        a = (u + 1.0) * (g * jax.nn.sigmoid(SWIGLU_ALPHA * g))
        return a @ w_dn[e] + b_dn[e]

    yb = lax.map(expert_block, (xb, block_e))
    y = yb.reshape(n_slots, D_MODEL) * slot_gate[:, None]
    out = jnp.zeros((T + 1, D_MODEL), y.dtype).at[slot_tok].add(y)[:T]
    return out.reshape(*lead, D_MODEL)


def setup_inputs(seed: int = 0) -> dict:
    key = jax.random.key(seed)
    ks = jax.random.split(key, 29)

    def nrm(k, shape, scale=1.0):
        return scale * jax.random.normal(k, shape, jnp.float32)

    L = DEPTH
    return {
        'x_prompt': nrm(ks[0], (BATCH, SEQ, D_MODEL)),
        'x_sample': nrm(ks[1], (DEC_BATCH, DEC_SEQ, D_MODEL)),
        'mem_prompt': nrm(ks[2], (BATCH, N_MEM, D_MODEL)),
        'cache_k': nrm(ks[3], (L, DEC_BATCH, PAST_LEN, N_KV_A, HEAD_DIM_A)),
        'cache_v': nrm(ks[4], (L, DEC_BATCH, PAST_LEN, N_KV_A, HEAD_DIM_A)),
        'cache_kidx': nrm(ks[5], (L, DEC_BATCH, PAST_LEN, IDX_DIM)),
        'cache_mem_k': nrm(ks[6], (L, DEC_BATCH, N_MEM, N_HEADS_M, HEAD_DIM_M)),
        'cache_mem_v': nrm(ks[7], (L, DEC_BATCH, N_MEM, N_HEADS_M, HEAD_DIM_M)),
        'state_gla': nrm(ks[8], (L, DEC_BATCH, N_HEADS_G, DK_G, DV_G)),
        'g_mix_norm': 1.0 + nrm(ks[9], (L, D_MODEL), 0.1),
        'w_in': nrm(ks[10], (L, D_MODEL, IN_WIDTH), D_MODEL ** -0.5),
        'b_gate': nrm(ks[11], (L, N_BRANCH * D_MODEL), 0.1),
        'w_gla_rank': nrm(ks[12], (L, GATE_RANK, W_KG), GATE_RANK ** -0.5),
        'b_gla_rank': nrm(ks[13], (L, W_KG), 0.1),
        'g_gla_norm': 1.0 + nrm(ks[14], (L, W_VG), 0.1),
        'g_mem_norm': 1.0 + nrm(ks[15], (L, D_MODEL), 0.1),
        'w_mem_kv': nrm(ks[16], (L, D_MODEL, 2 * W_M), D_MODEL ** -0.5),
        'w_br_a': nrm(ks[17], (L, W_A, D_MODEL), W_A ** -0.5),
        'w_br_b': nrm(ks[18], (L, W_VG, D_MODEL), W_VG ** -0.5),
        'w_br_m': nrm(ks[19], (L, W_M, D_MODEL), W_M ** -0.5),
        'w_o': nrm(ks[20], (L, D_MODEL, D_MODEL), D_MODEL ** -0.5),
        'g_ffn_norm': 1.0 + nrm(ks[21], (L, D_MODEL), 0.1),
        'w_router': nrm(ks[22], (L, D_MODEL, N_EXPERTS), D_MODEL ** -0.5),
        'b_router': nrm(ks[23], (L, N_EXPERTS), 0.01),
        'w_gu': nrm(ks[24], (L, N_EXPERTS, D_MODEL, 2 * D_EXPERT), D_MODEL ** -0.5),
        'b_gu': nrm(ks[25], (L, N_EXPERTS, 2 * D_EXPERT), 0.01),
        'w_dn': nrm(ks[26], (L, N_EXPERTS, D_EXPERT, D_MODEL), D_EXPERT ** -0.5),
        'b_dn': nrm(ks[27], (L, N_EXPERTS, D_MODEL), 0.01),
        'g_final': 1.0 + nrm(ks[28], (D_MODEL,), 0.1),
    }


def reference(x_prompt, x_sample, mem_prompt, cache_k, cache_v, cache_kidx, cache_mem_k, cache_mem_v, state_gla,
              g_mix_norm, w_in, b_gate, w_gla_rank, b_gla_rank, g_gla_norm, g_mem_norm, w_mem_kv,
              w_br_a, w_br_b, w_br_m, w_o, g_ffn_norm, w_router, b_router, w_gu, b_gu, w_dn, b_dn, g_final):
    T_p = x_prompt.shape[1]
    T_s = x_sample.shape[1]
    P = cache_k.shape[2]
    topk_p = min(TOPK_MAX, T_p // 4)
    topk_s = min(TOPK_MAX, (P + T_s) // 4)
    pos_p = jnp.arange(T_p, dtype=jnp.int32)
    pos_s = P + jnp.arange(T_s, dtype=jnp.int32)
    pos_all_s = jnp.arange(P + T_s, dtype=jnp.int32)
    xp, xs = x_prompt, x_sample
    nk_p, nv_p, nki_p, nmk_p, nmv_p, ns_p = [], [], [], [], [], []
    nk_s, nv_s, nki_s, ns_s = [], [], [], []
    for l in range(DEPTH):
        h = rms_norm(xp, g_mix_norm[l])
        qa, ka, va, qi, ki, wi, qg, kg, vg, lg, og, qm, gates = in_projection(h, pos_p, w_in[l], b_gate[l], w_gla_rank[l], b_gla_rank[l])
        o_a = dsa_prompt(qa, ka, va, qi, ki, wi, pos_p, topk_p)
        S_p, o_g = gla_prompt(qg, kg, vg, lg)
        mk, mv = memory_kv(mem_prompt, g_mem_norm[l], w_mem_kv[l])
        o_m = mem_attend(qm, mk, mv)
        xp = merge_branches(xp, o_a, gla_output(o_g, og, g_gla_norm[l]), o_m, gates, w_br_a[l], w_br_b[l], w_br_m[l], w_o[l])
        xp = xp + moe_ffn(rms_norm(xp, g_ffn_norm[l]), w_router[l], b_router[l], w_gu[l], b_gu[l], w_dn[l], b_dn[l])
        nk_p.append(ka)
        nv_p.append(va)
        nki_p.append(ki)
        nmk_p.append(mk)
        nmv_p.append(mv)
        ns_p.append(S_p)
        h = rms_norm(xs, g_mix_norm[l])
        qa, ka, va, qi, ki, wi, qg, kg, vg, lg, og, qm, gates = in_projection(h, pos_s, w_in[l], b_gate[l], w_gla_rank[l], b_gla_rank[l])
        k_all = jnp.concatenate([cache_k[l], ka], axis=1)
        v_all = jnp.concatenate([cache_v[l], va], axis=1)
        ki_all = jnp.concatenate([cache_kidx[l], ki], axis=1)
        o_a = dsa_attend(qa, qi, wi, pos_s, k_all, v_all, ki_all, pos_all_s, topk_s)
        S_s, o_g = gla_chunk(state_gla[l], (qg, kg, vg, lg))
        o_m = mem_attend(qm, cache_mem_k[l], cache_mem_v[l])
        xs = merge_branches(xs, o_a, gla_output(o_g, og, g_gla_norm[l]), o_m, gates, w_br_a[l], w_br_b[l], w_br_m[l], w_o[l])
        xs = xs + moe_ffn(rms_norm(xs, g_ffn_norm[l]), w_router[l], b_router[l], w_gu[l], b_gu[l], w_dn[l], b_dn[l])
        nk_s.append(ka)
        nv_s.append(va)
        nki_s.append(ki)
        ns_s.append(S_s)
    y_prompt = rms_norm(xp, g_final)
    y_sample = rms_norm(xs, g_final)
    return (y_prompt, y_sample, jnp.stack(nk_p), jnp.stack(nv_p), jnp.stack(nki_p), jnp.stack(nmk_p), jnp.stack(nmv_p), jnp.stack(ns_p), jnp.stack(nk_s), jnp.stack(nv_s), jnp.stack(nki_s), jnp.stack(ns_s))
```

```python
import functools

import jax
import jax.numpy as jnp
from jax import lax
from jax.experimental import pallas as pl
from jax.experimental.pallas import tpu as pltpu

D_MODEL = 1024
CHUNK = 64
ROPE_THETA = 10000.0
EPS = 1e-6
N_HEADS_A = 8
N_KV_A = 2
HEAD_DIM_A = 64
N_HEADS_IDX = 8
IDX_DIM = 64
TOPK_MAX = 256
N_HEADS_G = 4
DK_G = 64
DV_G = 128
GATE_RANK = 16
GATE_TAU = 16.0
N_MEM = 256
N_HEADS_M = 4
HEAD_DIM_M = 128
N_EXPERTS = 32
TOP_K = 4
SWIGLU_LIMIT = 7.0
SWIGLU_ALPHA = 1.702
N_BRANCH = 3

W_A = N_HEADS_A * HEAD_DIM_A
W_KVA = N_KV_A * HEAD_DIM_A
W_QI = N_HEADS_IDX * IDX_DIM
W_KG = N_HEADS_G * DK_G
W_VG = N_HEADS_G * DV_G
W_M = N_HEADS_M * HEAD_DIM_M
IN_WIDTHS = (W_A, W_KVA, W_KVA, W_QI, IDX_DIM, N_HEADS_IDX, W_KG, W_KG, W_VG, GATE_RANK, W_VG, W_M,
             N_BRANCH * D_MODEL)

LANES = 128
ROW_TILE = 256
KEY_BLOCK = 256
EXPERT_ROWS = 512
VMEM_LIMIT = 56 * 1024 * 1024
INT_MIN = -2147483648

F32 = jnp.float32
BF16 = jnp.bfloat16
I32 = jnp.int32


def _cparams(sem):
    return pltpu.CompilerParams(dimension_semantics=sem, vmem_limit_bytes=VMEM_LIMIT)


def _dot(a, b):
    return jnp.dot(a, b, preferred_element_type=F32)


def _dot_nt(a, b):
    return lax.dot_general(a, b, (((1,), (1,)), ((), ())), preferred_element_type=F32)


def _lane_iota(shape):
    return lax.broadcasted_iota(I32, shape, len(shape) - 1)


def _row_iota(shape):
    return lax.broadcasted_iota(I32, shape, len(shape) - 2)


def _rope_slab(z, cos_t, sin_s):
    w = z.shape[1]
    reps = w // LANES
    cos_w = jnp.concatenate([cos_t] * reps, axis=1) if reps > 1 else cos_t
    sin_w = jnp.concatenate([sin_s] * reps, axis=1) if reps > 1 else sin_s
    first = (_lane_iota(z.shape) % HEAD_DIM_A) < (HEAD_DIM_A // 2)
    nxt = pltpu.roll(z, w - HEAD_DIM_A // 2, axis=1)
    prv = pltpu.roll(z, HEAD_DIM_A // 2, axis=1)
    return z * cos_w + jnp.where(first, nxt, prv) * sin_w


def _inproj_kernel(x_ref, g_ref, cos_ref, sin_ref, wrope_ref, wplain_ref, wsmall_ref, wgate_ref, bgate_ref,
                   wrank_ref, brank_ref,
                   qa_ref, qi_ref, ka_ref, kpad_ref, ki_ref, kipad_ref, va_ref, vpad_ref, wi_ref,
                   qg_ref, kg_ref, vg_ref, lg_ref, og_ref, qm_ref, gates_ref, h_ref):
    x = x_ref[...]
    y = x * lax.rsqrt(jnp.mean(x * x, axis=-1, keepdims=True) + EPS)
    h_ref[...] = (y * g_ref[...]).astype(BF16)
    cos_t = cos_ref[...]
    sin_s = sin_ref[...]
    lo = _lane_iota((x.shape[0], LANES)) < HEAD_DIM_A

    qa = _rope_slab(_dot(h_ref[...], wrope_ref[:, 0:512]), cos_t, sin_s)
    qa_ref[...] = (qa * (HEAD_DIM_A ** -0.5)).astype(BF16)
    qi = _rope_slab(_dot(h_ref[...], wrope_ref[:, 512:1024]), cos_t, sin_s)
    qi_ref[...] = qi.astype(BF16)
    kk = _rope_slab(_dot(h_ref[...], wrope_ref[:, 1024:1280]), cos_t, sin_s)
    ka = kk[:, 0:LANES]
    ka_ref[...] = ka
    ka_sw = pltpu.roll(ka, HEAD_DIM_A, axis=1)
    zero = jnp.zeros_like(ka)
    kpad_ref[:, 0:128] = jnp.where(lo, ka, zero).astype(BF16)
    kpad_ref[:, 128:256] = jnp.where(lo, zero, ka_sw).astype(BF16)
    kpad_ref[:, 256:384] = jnp.where(lo, ka_sw, zero).astype(BF16)
    kpad_ref[:, 384:512] = jnp.where(lo, zero, ka).astype(BF16)
    kis = kk[:, LANES:2 * LANES]
    ki_ref[...] = kis[:, 0:IDX_DIM]
    kipad_ref[:, 0:128] = kis.astype(BF16)
    kipad_ref[:, 128:256] = pltpu.roll(kis, IDX_DIM, axis=1).astype(BF16)

    vq = _dot(h_ref[...], wplain_ref[:, 0:640])
    va = vq[:, 0:128]
    va_ref[...] = va
    vpad_ref[:, 0:128] = va.astype(BF16)
    vpad_ref[:, 128:256] = pltpu.roll(va, HEAD_DIM_A, axis=1).astype(BF16)
    qg_ref[...] = vq[:, 128:384] * (DK_G ** -0.5)
    kg_ref[...] = vq[:, 384:640]
    vg_ref[...] = _dot(h_ref[...], wplain_ref[:, 640:1152])
    og_ref[...] = _dot(h_ref[...], wplain_ref[:, 1152:1664])
    qm_ref[...] = _dot(h_ref[...], wplain_ref[:, 1664:2176]).astype(BF16)

    sm = _dot(h_ref[...], wsmall_ref[...])
    wi_ref[...] = sm[:, 0:128] * (N_HEADS_IDX ** -0.5 * IDX_DIM ** -0.5)
    pre = _dot(sm[:, 128:256].astype(BF16), wrank_ref[...]) + brank_ref[...]
    log_sig = jnp.minimum(pre, 0.0) - jnp.log(1.0 + jnp.exp(-jnp.abs(pre)))
    lg_ref[...] = log_sig / GATE_TAU

    for c in range(N_BRANCH * D_MODEL // 512):
        gt = _dot(h_ref[...], wgate_ref[:, c * 512:(c + 1) * 512]) + bgate_ref[:, c * 512:(c + 1) * 512]
        gates_ref[:, c * 512:(c + 1) * 512] = (1.0 / (1.0 + jnp.exp(-gt))).astype(BF16)


def _split_w_in(w_in):
    cuts, acc = [], 0
    for w in IN_WIDTHS:
        cuts.append((acc, acc + w))
        acc += w
    return [w_in[:, a:b] for a, b in cuts]


def _inproj(x_all, cos_t, sin_s, g_mix, w_in, b_gate, w_rank, b_rank):
    n = x_all.shape[0]
    qa, ka, va, qi, ki, wi, qg, kg, vg, ag, og, qm, gt = _split_w_in(w_in)
    zc = lambda k: jnp.zeros((D_MODEL, k), F32)
    wrope = jnp.concatenate([qa, qi, ka, ki, zc(LANES - IDX_DIM)], axis=1).astype(BF16)
    wplain = jnp.concatenate([va, qg, kg, vg, og, qm], axis=1).astype(BF16)
    wsmall = jnp.concatenate([wi, zc(LANES - N_HEADS_IDX), ag, zc(LANES - GATE_RANK)], axis=1).astype(BF16)
    wgate = gt.astype(BF16)
    wrank = jnp.concatenate([w_rank, jnp.zeros((LANES - GATE_RANK, W_KG), F32)], axis=0).astype(BF16)

    tm = ROW_TILE
    row = lambda w: pl.BlockSpec((tm, w), lambda i: (i, 0))
    full = lambda a: pl.BlockSpec(a.shape, lambda i: (0, 0))
    outs = [("qa", 512, BF16), ("qi", 512, BF16), ("ka", 128, F32), ("kpad", 512, BF16), ("ki", 64, F32),
            ("kipad", 256, BF16), ("va", 128, F32), ("vpad", 256, BF16), ("wi", 128, F32), ("qg", 256, F32),
            ("kg", 256, F32), ("vg", 512, F32), ("lg", 256, F32), ("og", 512, F32), ("qm", 512, BF16),
            ("gates", N_BRANCH * D_MODEL, BF16)]
    g2 = g_mix.reshape(1, D_MODEL)
    bg2 = b_gate.reshape(1, -1)
    br2 = b_rank.reshape(1, -1)
    res = pl.pallas_call(
        _inproj_kernel,
        grid=(n // tm,),
        in_specs=[row(D_MODEL), full(g2), row(LANES), row(LANES), full(wrope), full(wplain), full(wsmall),
                  full(wgate), full(bg2), full(wrank), full(br2)],
        out_specs=[row(w) for _, w, _ in outs],
        out_shape=[jax.ShapeDtypeStruct((n, w), dt) for _, w, dt in outs],
        scratch_shapes=[pltpu.VMEM((tm, D_MODEL), BF16)],
        compiler_params=_cparams(("parallel",)),
        name="inproj",
    )(x_all, g2, cos_t, sin_s, wrope, wplain, wsmall, wgate, bg2, wrank, br2)
    return {name: r for (name, _, _), r in zip(outs, res)}


def _rope_tables(pos):
    half = HEAD_DIM_A // 2
    inv_freq = ROPE_THETA ** (-jnp.arange(half, dtype=F32) / half)
    ang = pos.astype(F32)[:, None] * inv_freq[None, :]
    cos = jnp.cos(ang)
    sin = jnp.sin(ang)
    cos_t = jnp.concatenate([cos, cos, cos, cos], axis=1)
    sin_s = jnp.concatenate([-sin, sin, -sin, sin], axis=1)
    return cos_t, sin_s


def _dsa_kernel(qi_ref, wi_ref, qa_ref, kipad_ref, kpad_ref, vpad_ref, o_ref,
                key_ref, bias_ref, l_ref, wb_ref, *, tq, nkb_max, topk, qpos_fn, nkb_fn, l_valid):
    j = pl.program_id(1)
    nkb = nkb_fn(j)
    qpos0 = qpos_fn(j)
    kb_w = KEY_BLOCK

    for hh in range(N_HEADS_IDX):
        wcol = jnp.broadcast_to(wi_ref[:, hh:hh + 1], (tq, LANES))
        wb_ref[hh] = jnp.concatenate([wcol, wcol], axis=1)

    qchunk = (qpos0 + _row_iota((tq, kb_w))) // CHUNK

    def score_body(kb, carry):
        k0 = pl.multiple_of(kb * kb_w, kb_w)
        ka_blk = kipad_ref[pl.ds(k0, kb_w), 0:128]
        kb_blk = kipad_ref[pl.ds(k0, kb_w), 128:256]
        s = jnp.zeros((tq, kb_w), F32)
        for p in range(N_HEADS_IDX // 2):
            lhs = qi_ref[:, p * LANES:(p + 1) * LANES]
            s = s + jnp.maximum(_dot_nt(lhs, ka_blk), 0.0) * wb_ref[2 * p]
            s = s + jnp.maximum(_dot_nt(lhs, kb_blk), 0.0) * wb_ref[2 * p + 1]
        bits = lax.bitcast_convert_type(s, I32)
        key = bits ^ ((bits >> 31) & 0x7FFFFFFF)
        kidx = k0 + _lane_iota((tq, kb_w))
        adm = jnp.logical_and(kidx // CHUNK <= qchunk, kidx < l_valid)
        key_ref[kb] = jnp.where(adm, key, INT_MIN)
        return carry

    lax.fori_loop(0, nkb, score_body, 0)

    def count(pred_fn):
        def body(kb, acc):
            blk = key_ref[kb]
            kidx = kb * kb_w + _lane_iota((tq, kb_w))
            c = jnp.where(pred_fn(blk, kidx), 1.0, 0.0)
            return acc + c[:, 0:LANES] + c[:, LANES:2 * LANES]
        acc = lax.fori_loop(0, nkb, body, jnp.zeros((tq, LANES), F32))
        return jnp.sum(acc, axis=1, keepdims=True)

    def wide(v):
        return jnp.concatenate([v, v], axis=1)

    kf = float(topk)

    def search():
        c0 = count(lambda blk, kidx: blk >= 0)
        thr0 = jnp.where(jnp.broadcast_to(c0, (tq, LANES)) >= kf, 0, INT_MIN).astype(I32)

        def bit_body(i, thr):
            cand = thr + jnp.left_shift(jnp.int32(1), 30 - i)
            cand_w = wide(cand)
            c = count(lambda blk, kidx: blk >= cand_w)
            return jnp.where(jnp.broadcast_to(c, (tq, LANES)) >= kf, cand, thr)

        return lax.fori_loop(0, 31, bit_body, thr0)

    thr = lax.cond(nkb * kb_w > topk, search, lambda: jnp.full((tq, LANES), INT_MIN, I32))
    thr_w = wide(thr)

    n_gt = count(lambda blk, kidx: blk > thr_w)
    n_ge = count(lambda blk, kidx: blk >= thr_w)
    need = jnp.broadcast_to(kf - n_gt, (tq, LANES))
    idx_cap = nkb_max * kb_w

    def tie_search():
        nbits = max(1, (idx_cap - 1).bit_length())

        def bit_body(i, c):
            cand = c + jnp.left_shift(jnp.int32(1), nbits - 1 - i)
            cand_w = wide(cand)
            cnt = count(lambda blk, kidx: jnp.logical_and(blk == thr_w, kidx < cand_w))
            return jnp.where(jnp.broadcast_to(cnt, (tq, LANES)) <= need, cand, c)

        return lax.fori_loop(0, nbits, bit_body, jnp.zeros((tq, LANES), I32))

    has_ties = jnp.max(n_ge) > kf
    cut_w = wide(lax.cond(has_ties, tie_search, lambda: jnp.full((tq, LANES), idx_cap, I32)))

    def bias_body(kb, carry):
        blk = key_ref[kb]
        kidx = kb * kb_w + _lane_iota((tq, kb_w))
        sel = jnp.logical_or(blk > thr_w, jnp.logical_and(blk == thr_w, kidx < cut_w))
        sel = jnp.logical_and(sel, blk != INT_MIN)
        bias_ref[kb] = jnp.where(sel, 0.0, -jnp.inf)
        return carry

    lax.fori_loop(0, nkb, bias_body, 0)

    lo = _lane_iota((tq, LANES)) < HEAD_DIM_A
    for p in range(N_HEADS_A // 2):
        n = (2 * p) // (N_HEADS_A // N_KV_A)
        slab = qa_ref[:, p * LANES:(p + 1) * LANES]
        halves = []
        for e in range(2):
            kcol = (2 * n + e) * LANES
            vcol = LANES if n != e else 0

            def logit_body(kb, m, kcol=kcol):
                k0 = pl.multiple_of(kb * kb_w, kb_w)
                kblk = kpad_ref[pl.ds(k0, kb_w), kcol:kcol + LANES]
                lg = _dot_nt(slab, kblk) + bias_ref[kb]
                l_ref[kb] = lg
                return jnp.maximum(m, jnp.maximum(lg[:, 0:LANES], lg[:, LANES:2 * LANES]))

            m = lax.fori_loop(0, nkb, logit_body, jnp.full((tq, LANES), -jnp.inf, F32))
            mrow = jnp.max(m, axis=1, keepdims=True)

            def pv_body(kb, carry, vcol=vcol):
                ssum, acc = carry
                k0 = pl.multiple_of(kb * kb_w, kb_w)
                ex = jnp.exp(l_ref[kb] - mrow)
                ssum = ssum + ex[:, 0:LANES] + ex[:, LANES:2 * LANES]
                vblk = vpad_ref[pl.ds(k0, kb_w), vcol:vcol + LANES]
                acc = acc + _dot(ex.astype(BF16), vblk)
                return ssum, acc

            ssum, acc = lax.fori_loop(0, nkb, pv_body,
                                      (jnp.zeros((tq, LANES), F32), jnp.zeros((tq, LANES), F32)))
            halves.append(acc / jnp.sum(ssum, axis=1, keepdims=True))
        o_ref[:, p * LANES:(p + 1) * LANES] = jnp.where(lo, halves[0], halves[1]).astype(BF16)


def _dsa(qi, wi, qa, kipad, kpad, vpad, *, batch, tq, n_q_tiles, q_block0, k_rows, topk, qpos_fn, nkb_fn,
         l_valid, name):
    nkb_max = k_rows // KEY_BLOCK
    qmap = lambda b, j: (q_block0 + b * n_q_tiles + j, 0)
    kmap = lambda b, j: (b, 0)
    kern = functools.partial(_dsa_kernel, tq=tq, nkb_max=nkb_max, topk=topk, qpos_fn=qpos_fn, nkb_fn=nkb_fn,
                             l_valid=l_valid)
    return pl.pallas_call(
        kern,
        grid=(batch, n_q_tiles),
        in_specs=[pl.BlockSpec((tq, W_QI), qmap), pl.BlockSpec((tq, LANES), qmap), pl.BlockSpec((tq, W_A), qmap),
                  pl.BlockSpec((k_rows, 256), kmap), pl.BlockSpec((k_rows, 512), kmap),
                  pl.BlockSpec((k_rows, 256), kmap)],
        out_specs=pl.BlockSpec((tq, W_A), lambda b, j: (b * n_q_tiles + j, 0)),
        out_shape=jax.ShapeDtypeStruct((batch * n_q_tiles * tq, W_A), BF16),
        scratch_shapes=[pltpu.VMEM((nkb_max, tq, KEY_BLOCK), I32), pltpu.VMEM((nkb_max, tq, KEY_BLOCK), F32),
                        pltpu.VMEM((nkb_max, tq, KEY_BLOCK), F32), pltpu.VMEM((N_HEADS_IDX, tq, KEY_BLOCK), F32)],
        compiler_params=_cparams(("parallel", "arbitrary")),
        name=name,
    )(qi, wi, qa, kipad, kpad, vpad)


def _cumsum_rows(x):
    c = x.shape[0]
    rows = _row_iota(x.shape)
    s = 1
    while s < c:
        x = x + jnp.where(rows >= s, pltpu.roll(x, s, axis=0), 0.0)
        s *= 2
    return x


def _gla_kernel(q_ref, k_ref, lg_ref, v_ref, og_ref, gn_ref, s0_ref, o_ref, sout_ref, st_ref, *, tt, chunk):
    t = pl.program_id(1)
    n_pairs = N_HEADS_G // 2

    @pl.when(t == 0)
    def _():
        for p in range(n_pairs):
            pair = jnp.concatenate([s0_ref[2 * p], s0_ref[2 * p + 1]], axis=0)
            st_ref[p] = pair.T

    lane = _lane_iota((chunk, LANES))
    masks = (lane < DK_G, lane >= DK_G)
    causal = _row_iota((chunk, chunk)) >= _lane_iota((chunk, chunk))
    pad_rows = LANES - chunk

    for c in range(tt // chunk):
        r0 = c * chunk
        bcum = _cumsum_rows(lg_ref[r0:r0 + chunk, :])
        for p in range(n_pairs):
            sl = slice(p * LANES, (p + 1) * LANES)
            b = bcum[:, sl]
            q = q_ref[r0:r0 + chunk, sl]
            k = k_ref[r0:r0 + chunk, sl]
            b_last = b[chunk - 1:chunk, :]
            b_mid = b[chunk // 2:chunk // 2 + 1, :]
            qb = q * jnp.exp(b)
            qe = (q * jnp.exp(b - b_mid)).astype(BF16)
            ke = k * jnp.exp(b_mid - b)
            kd = k * jnp.exp(b_last - b)
            st_old = st_ref[p]
            st_bf = st_old.astype(BF16)
            upd = jnp.zeros((LANES, LANES), F32)
            for e in range(2):
                hd = 2 * p + e
                vh = v_ref[r0:r0 + chunk, hd * DV_G:(hd + 1) * DV_G]
                a = _dot_nt(qe, jnp.where(masks[e], ke, 0.0).astype(BF16))
                a = jnp.where(causal, a, 0.0)
                o = _dot(a.astype(BF16), vh.astype(BF16))
                o = o + _dot_nt(jnp.where(masks[e], qb, 0.0).astype(BF16), st_bf)
                vh_sq = jnp.concatenate([vh, jnp.zeros((pad_rows, DV_G), F32)], axis=0) if pad_rows else vh
                kd_m = jnp.where(masks[e], kd, 0.0)
                kd_sq = jnp.concatenate([kd_m, jnp.zeros((pad_rows, LANES), F32)], axis=0) if pad_rows else kd_m
                upd = upd + _dot(vh_sq.T.astype(BF16), kd_sq.astype(BF16))
                on = o * lax.rsqrt(jnp.mean(o * o, axis=-1, keepdims=True) + EPS)
                on = on * gn_ref[:, hd * DV_G:(hd + 1) * DV_G]
                og = og_ref[r0:r0 + chunk, hd * DV_G:(hd + 1) * DV_G]
                o_ref[r0:r0 + chunk, hd * DV_G:(hd + 1) * DV_G] = (on * (og / (1.0 + jnp.exp(-og)))).astype(BF16)
            st_ref[p] = st_old * jnp.exp(b_last) + upd

    @pl.when(t == pl.num_programs(1) - 1)
    def _():
        for p in range(n_pairs):
            pair = st_ref[p].T
            sout_ref[2 * p] = pair[0:DK_G, :]
            sout_ref[2 * p + 1] = pair[DK_G:2 * DK_G, :]


def _gla(qg, kg, lg, vg, og, g_norm, s0, *, batch, tt, n_tiles, block0, chunk, name):
    rmap = lambda b, t: (block0 + b * n_tiles + t, 0)
    smap = lambda b, t: (b, 0, 0, 0)
    gn2 = g_norm.reshape(1, W_VG)
    kern = functools.partial(_gla_kernel, tt=tt, chunk=chunk)
    return pl.pallas_call(
        kern,
        grid=(batch, n_tiles),
        in_specs=[pl.BlockSpec((tt, W_KG), rmap), pl.BlockSpec((tt, W_KG), rmap), pl.BlockSpec((tt, W_KG), rmap),
                  pl.BlockSpec((tt, W_VG), rmap), pl.BlockSpec((tt, W_VG), rmap),
                  pl.BlockSpec((1, W_VG), lambda b, t: (0, 0)),
                  pl.BlockSpec((None, N_HEADS_G, DK_G, DV_G), smap)],
        out_specs=[pl.BlockSpec((tt, W_VG), lambda b, t: (b * n_tiles + t, 0)),
                   pl.BlockSpec((None, N_HEADS_G, DK_G, DV_G), smap)],
        out_shape=[jax.ShapeDtypeStruct((batch * n_tiles * tt, W_VG), BF16),
                   jax.ShapeDtypeStruct((batch, N_HEADS_G, DK_G, DV_G), F32)],
        scratch_shapes=[pltpu.VMEM((N_HEADS_G // 2, LANES, LANES), F32)],
        compiler_params=_cparams(("parallel", "arbitrary")),
        name=name,
    )(qg, kg, lg, vg, og, gn2, s0)


def _memkv_kernel(m_ref, g_ref, w_ref, mk_ref, mv_ref):
    x = m_ref[...]
    y = x * lax.rsqrt(jnp.mean(x * x, axis=-1, keepdims=True) + EPS)
    h = (y * g_ref[...]).astype(BF16)
    mk_ref[...] = _dot(h, w_ref[:, 0:W_M])
    mv_ref[...] = _dot(h, w_ref[:, W_M:2 * W_M])


def _memkv(mem_rows, g_mem, w_mem_kv):
    n = mem_rows.shape[0]
    tm = ROW_TILE
    g2 = g_mem.reshape(1, D_MODEL)
    wb = w_mem_kv.astype(BF16)
    return pl.pallas_call(
        _memkv_kernel,
        grid=(n // tm,),
        in_specs=[pl.BlockSpec((tm, D_MODEL), lambda i: (i, 0)), pl.BlockSpec((1, D_MODEL), lambda i: (0, 0)),
                  pl.BlockSpec(wb.shape, lambda i: (0, 0))],
        out_specs=[pl.BlockSpec((tm, W_M), lambda i: (i, 0))] * 2,
        out_shape=[jax.ShapeDtypeStruct((n, W_M), F32)] * 2,
        compiler_params=_cparams(("parallel",)),
        name="memkv",
    )(mem_rows, g2, wb)


def _memattn_kernel(q_ref, mk_ref, mv_ref, o_ref):
    for hd in range(N_HEADS_M):
        sl = slice(hd * HEAD_DIM_M, (hd + 1) * HEAD_DIM_M)
        logits = _dot_nt(q_ref[:, sl], mk_ref[:, sl].astype(BF16)) * (HEAD_DIM_M ** -0.5)
        m = jnp.max(logits, axis=-1, keepdims=True)
        ex = jnp.exp(logits - m)
        o = _dot(ex.astype(BF16), mv_ref[:, sl].astype(BF16))
        o_ref[:, sl] = (o / jnp.sum(ex, axis=-1, keepdims=True)).astype(BF16)


def _memattn(qm, mk, mv, *, batch, tq, n_tiles, block0, name):
    qmap = lambda b, t: (block0 + b * n_tiles + t, 0)
    kmap = lambda b, t: (b, 0)
    return pl.pallas_call(
        _memattn_kernel,
        grid=(batch, n_tiles),
        in_specs=[pl.BlockSpec((tq, W_M), qmap), pl.BlockSpec((N_MEM, W_M), kmap),
                  pl.BlockSpec((N_MEM, W_M), kmap)],
        out_specs=pl.BlockSpec((tq, W_M), lambda b, t: (b * n_tiles + t, 0)),
        out_shape=jax.ShapeDtypeStruct((batch * n_tiles * tq, W_M), BF16),
        compiler_params=_cparams(("parallel", "arbitrary")),
        name=name,
    )(qm, mk, mv)


def _merge_kernel(x_ref, oa_ref, ob_ref, om_ref, gates_ref, wa_ref, wb_ref, wm_ref, wo_ref, gf_ref, wr_ref,
                  br_ref, x2_ref, hn_ref, rgate_ref, rcode_ref, cnt_ref, run_ref, *, shift):
    i = pl.program_id(0)
    tm = x_ref.shape[0]

    @pl.when(i == 0)
    def _():
        run_ref[...] = jnp.zeros_like(run_ref)

    d = D_MODEL
    m = gates_ref[:, 0:d].astype(F32) * _dot(oa_ref[...], wa_ref[...])
    m = m + gates_ref[:, d:2 * d].astype(F32) * _dot(ob_ref[...], wb_ref[...])
    m = m + gates_ref[:, 2 * d:3 * d].astype(F32) * _dot(om_ref[...], wm_ref[...])
    x2 = x_ref[...] + _dot(m.astype(BF16), wo_ref[...])
    x2_ref[...] = x2
    hn = x2 * lax.rsqrt(jnp.mean(x2 * x2, axis=-1, keepdims=True) + EPS) * gf_ref[...]
    hn_ref[...] = hn

    lane = _lane_iota((tm, LANES))
    lane_f = lane.astype(F32)
    logits = _dot(hn.astype(BF16), wr_ref[...]) + br_ref[...]
    work = jnp.where(lane < N_EXPERTS, logits, -jnp.inf)
    vals, hots, eidx = [], [], []
    for _ in range(TOP_K):
        mx = jnp.max(work, axis=-1, keepdims=True)
        idx = jnp.min(jnp.where(work == mx, lane_f, float(LANES)), axis=-1, keepdims=True)
        hot = lane_f == idx
        vals.append(mx)
        hots.append(hot)
        eidx.append(idx)
        work = jnp.where(hot, -jnp.inf, work)
    exps = [jnp.exp(v - vals[0]) for v in vals]
    den = exps[0] + exps[1] + exps[2] + exps[3]

    multi = jnp.where(jnp.logical_or(jnp.logical_or(hots[0], hots[1]), jnp.logical_or(hots[2], hots[3])), 1.0, 0.0)
    tri = jnp.where(_row_iota((tm, tm)) > _lane_iota((tm, tm)), 1.0, 0.0).astype(BF16)
    before = _dot(tri, multi.astype(BF16)) + run_ref[...]
    run_ref[...] = run_ref[...] + jnp.sum(multi, axis=0, keepdims=True)
    cnt_ref[...] = run_ref[...]

    rgate = jnp.zeros((tm, LANES), F32)
    rcode = jnp.zeros((tm, LANES), F32)
    for jj in range(TOP_K):
        rank = jnp.sum(jnp.where(hots[jj], before, 0.0), axis=-1, keepdims=True)
        sel = lane == jj
        rgate = jnp.where(sel, exps[jj] / den, rgate)
        rcode = jnp.where(sel, eidx[jj] * float(1 << shift) + rank, rcode)
    rgate_ref[...] = rgate
    rcode_ref[...] = rcode.astype(I32)


def _merge(x_all, oa, ob, om, gates, w_br_a, w_br_b, w_br_m, w_o, g_ffn, w_router, b_router, *, shift):
    n = x_all.shape[0]
    tm = ROW_TILE
    row = lambda w: pl.BlockSpec((tm, w), lambda i: (i, 0))
    full = lambda a: pl.BlockSpec(a.shape, lambda i: (0, 0))
    wa, wb, wm, wo = (w.astype(BF16) for w in (w_br_a, w_br_b, w_br_m, w_o))
    gf2 = g_ffn.reshape(1, D_MODEL)
    wr = jnp.concatenate([w_router, jnp.zeros((D_MODEL, LANES - N_EXPERTS), F32)], axis=1).astype(BF16)
    br = jnp.concatenate([b_router, jnp.zeros((LANES - N_EXPERTS,), F32)]).reshape(1, LANES)
    kern = functools.partial(_merge_kernel, shift=shift)
    return pl.pallas_call(
        kern,
        grid=(n // tm,),
        in_specs=[row(D_MODEL), row(W_A), row(W_VG), row(W_M), row(N_BRANCH * D_MODEL), full(wa), full(wb),
                  full(wm), full(wo), full(gf2), full(wr), full(br)],
        out_specs=[row(D_MODEL), row(D_MODEL), row(LANES), row(LANES),
                   pl.BlockSpec((1, LANES), lambda i: (0, 0))],
        out_shape=[jax.ShapeDtypeStruct((n, D_MODEL), F32), jax.ShapeDtypeStruct((n, D_MODEL), F32),
                   jax.ShapeDtypeStruct((n, LANES), F32), jax.ShapeDtypeStruct((n, LANES), I32),
                   jax.ShapeDtypeStruct((1, LANES), F32)],
        scratch_shapes=[pltpu.VMEM((1, LANES), F32)],
        compiler_params=_cparams(("arbitrary",)),
        name="merge",
    )(x_all, oa, ob, om, gates, wa, wb, wm, wo, gf2, wr, br)


def _slot(code, pstart_ref, shift):
    return pstart_ref[code >> shift] + (code & ((1 << shift) - 1))


def _dispatch_kernel(pstart_ref, cnt_ref, nu_ref, code_ref, hn_ref, xs_ref, zero_ref, sem, zsem, *, shift,
                     n_blocks):
    i = pl.program_id(0)
    tm = hn_ref.shape[0]
    er = EXPERT_ROWS

    def row_copy(r, jj):
        d = _slot(code_ref[0, 0, r * TOP_K + jj], pstart_ref, shift)
        return pltpu.make_async_copy(hn_ref.at[pl.ds(r, 1), :], xs_ref.at[pl.ds(d, 1), :], sem)

    def start_body(r, carry):
        for jj in range(TOP_K):
            row_copy(r, jj).start()
        return carry

    lax.fori_loop(0, tm, start_body, 0)

    @pl.when(i == pl.num_programs(0) - 1)
    def _():
        zero_ref[...] = jnp.zeros_like(zero_ref)

        def expert_body(e, carry):
            c = cnt_ref[e]
            lo_row = pstart_ref[e] + c
            hi_row = pstart_ref[e] + (c + er - 1) // er * er

            def zcopy(r):
                return pltpu.make_async_copy(zero_ref.at[pl.ds(0, 1), :], xs_ref.at[pl.ds(r, 1), :], zsem)

            def zstart(r, cc):
                zcopy(r).start()
                return cc

            def zwait(r, cc):
                zcopy(r).wait()
                return cc

            lax.fori_loop(lo_row, hi_row, zstart, 0)
            lax.fori_loop(lo_row, hi_row, zwait, 0)
            return carry

        lax.fori_loop(0, N_EXPERTS, expert_body, 0)

        def block_body(b, carry):
            cp = pltpu.make_async_copy(zero_ref, xs_ref.at[pl.ds(b * er, er), :], zsem)
            cp.start()
            cp.wait()
            return carry

        lax.fori_loop(nu_ref[0], n_blocks, block_body, 0)

    def wait_body(r, carry):
        for jj in range(TOP_K):
            row_copy(r, jj).wait()
        return carry

    lax.fori_loop(0, tm, wait_body, 0)


def _dispatch(pstart, counts, nused, code, hn, *, shift, n_blocks):
    n = hn.shape[0]
    tm = ROW_TILE
    code3 = code.reshape(n // tm, 1, tm * TOP_K)
    kern = functools.partial(_dispatch_kernel, shift=shift, n_blocks=n_blocks)
    return pl.pallas_call(
        kern,
        grid_spec=pltpu.PrefetchScalarGridSpec(
            num_scalar_prefetch=3,
            grid=(n // tm,),
            in_specs=[pl.BlockSpec((1, 1, tm * TOP_K), lambda i, *_: (i, 0, 0), memory_space=pltpu.SMEM),
                      pl.BlockSpec((tm, D_MODEL), lambda i, *_: (i, 0))],
            out_specs=pl.BlockSpec(memory_space=pl.ANY),
            scratch_shapes=[pltpu.VMEM((EXPERT_ROWS, D_MODEL), F32), pltpu.SemaphoreType.DMA,
                            pltpu.SemaphoreType.DMA],
        ),
        out_shape=jax.ShapeDtypeStruct((n_blocks * EXPERT_ROWS, D_MODEL), F32),
        compiler_params=_cparams(("arbitrary",)),
        name="dispatch",
    )(pstart, counts, nused, code3, hn)


def _experts_kernel(be_ref, nu_ref, x_ref, wgu_ref, bgu_ref, wdn_ref, bdn_ref, y_ref):
    i = pl.program_id(0)

    @pl.when(i < nu_ref[0])
    def _():
        xb = x_ref[...].astype(BF16)
        de = D_MODEL
        g = _dot(xb, wgu_ref[:, 0:de]) + bgu_ref[:, 0:de]
        u = _dot(xb, wgu_ref[:, de:2 * de]) + bgu_ref[:, de:2 * de]
        g = jnp.minimum(g, SWIGLU_LIMIT)
        u = jnp.clip(u, -SWIGLU_LIMIT, SWIGLU_LIMIT)
        a = (u + 1.0) * (g * (1.0 / (1.0 + jnp.exp(-SWIGLU_ALPHA * g))))
        y_ref[...] = _dot(a.astype(BF16), wdn_ref[...]) + bdn_ref[...]

    @pl.when(i >= nu_ref[0])
    def _():
        y_ref[...] = jnp.zeros_like(y_ref)


def _experts(be, nused, xs, w_gu, b_gu, w_dn, b_dn, *, n_blocks):
    wgu = w_gu.astype(BF16)
    wdn = w_dn.astype(BF16)
    bgu = b_gu.reshape(N_EXPERTS, 1, 2 * D_MODEL)
    bdn = b_dn.reshape(N_EXPERTS, 1, D_MODEL)
    er = EXPERT_ROWS
    return pl.pallas_call(
        _experts_kernel,
        grid_spec=pltpu.PrefetchScalarGridSpec(
            num_scalar_prefetch=2,
            grid=(n_blocks,),
            in_specs=[pl.BlockSpec((er, D_MODEL), lambda i, be, nu: (i, 0)),
                      pl.BlockSpec((None, D_MODEL, 2 * D_MODEL), lambda i, be, nu: (be[i], 0, 0)),
                      pl.BlockSpec((None, 1, 2 * D_MODEL), lambda i, be, nu: (be[i], 0, 0)),
                      pl.BlockSpec((None, D_MODEL, D_MODEL), lambda i, be, nu: (be[i], 0, 0)),
                      pl.BlockSpec((None, 1, D_MODEL), lambda i, be, nu: (be[i], 0, 0))],
            out_specs=pl.BlockSpec((er, D_MODEL), lambda i, be, nu: (i, 0)),
        ),
        out_shape=jax.ShapeDtypeStruct(xs.shape, F32),
        compiler_params=_cparams(("arbitrary",)),
        name="experts",
    )(be, nused, xs, wgu, bgu, wdn, bdn)


def _combine_kernel(pstart_ref, code_ref, x2_ref, gate_ref, gfin_ref, y_ref, out_ref, buf_ref, sem, *, shift,
                    final_norm):
    tm = x2_ref.shape[0]

    def row_copy(r, jj):
        d = _slot(code_ref[0, 0, r * TOP_K + jj], pstart_ref, shift)
        return pltpu.make_async_copy(y_ref.at[pl.ds(d, 1), :], buf_ref.at[jj, pl.ds(r, 1), :], sem)

    def start_body(r, carry):
        for jj in range(TOP_K):
            row_copy(r, jj).start()
        return carry

    def wait_body(r, carry):
        for jj in range(TOP_K):
            row_copy(r, jj).wait()
        return carry

    lax.fori_loop(0, tm, start_body, 0)
    lax.fori_loop(0, tm, wait_body, 0)

    acc = x2_ref[...]
    for jj in range(TOP_K):
        acc = acc + buf_ref[jj] * gate_ref[:, jj:jj + 1]
    if final_norm:
        acc = acc * lax.rsqrt(jnp.mean(acc * acc, axis=-1, keepdims=True) + EPS) * gfin_ref[...]
    out_ref[...] = acc


def _combine(pstart, code, x2, gate, g_final, ys, *, shift, final_norm):
    n = x2.shape[0]
    tm = ROW_TILE
    code3 = code.reshape(n // tm, 1, tm * TOP_K)
    gf2 = g_final.reshape(1, D_MODEL)
    return pl.pallas_call(
        functools.partial(_combine_kernel, shift=shift, final_norm=final_norm),
        grid_spec=pltpu.PrefetchScalarGridSpec(
            num_scalar_prefetch=1,
            grid=(n // tm,),
            in_specs=[pl.BlockSpec((1, 1, tm * TOP_K), lambda i, ps: (i, 0, 0), memory_space=pltpu.SMEM),
                      pl.BlockSpec((tm, D_MODEL), lambda i, ps: (i, 0)),
                      pl.BlockSpec((tm, LANES), lambda i, ps: (i, 0)),
                      pl.BlockSpec((1, D_MODEL), lambda i, ps: (0, 0)),
                      pl.BlockSpec(memory_space=pl.ANY)],
            out_specs=pl.BlockSpec((tm, D_MODEL), lambda i, ps: (i, 0)),
            scratch_shapes=[pltpu.VMEM((TOP_K, tm, D_MODEL), F32), pltpu.SemaphoreType.DMA],
        ),
        out_shape=jax.ShapeDtypeStruct((n, D_MODEL), F32),
        compiler_params=_cparams(("arbitrary",)),
        name="combine",
    )(pstart, code3, x2, gate, gf2, ys)


def _key_side_sample(cache_k, cache_v, cache_kidx, kpad_new, vpad_new, kipad_new, k_rows):
    db, past = cache_k.shape[0], cache_k.shape[1]
    ck = cache_k.reshape(db, past, W_KVA)
    cv = cache_v.reshape(db, past, W_KVA)
    z64 = jnp.zeros((db, past, HEAD_DIM_A), F32)
    k0, k1 = ck[..., :HEAD_DIM_A], ck[..., HEAD_DIM_A:]
    kpad_c = jnp.concatenate([k0, z64, z64, k0, k1, z64, z64, k1], axis=-1).astype(BF16)
    vpad_c = jnp.concatenate([cv, cv[..., HEAD_DIM_A:], cv[..., :HEAD_DIM_A]], axis=-1).astype(BF16)
    kipad_c = jnp.concatenate([cache_kidx, z64, z64, cache_kidx], axis=-1).astype(BF16)
    ds = kpad_new.shape[0] // db
    pad = k_rows - past - ds

    def cat(c, new):
        w = c.shape[-1]
        parts = [c, new.reshape(db, ds, w)]
        if pad:
            parts.append(jnp.zeros((db, pad, w), BF16))
        return jnp.concatenate(parts, axis=1).reshape(db * k_rows, w)

    return cat(kipad_c, kipad_new), cat(kpad_c, kpad_new), cat(vpad_c, vpad_new)


def kernel(x_prompt, x_sample, mem_prompt, cache_k, cache_v, cache_kidx, cache_mem_k, cache_mem_v, state_gla,
           g_mix_norm, w_in, b_gate, w_gla_rank, b_gla_rank, g_gla_norm, g_mem_norm, w_mem_kv, w_br_a, w_br_b,
           w_br_m, w_o, g_ffn_norm, w_router, b_router, w_gu, b_gu, w_dn, b_dn, g_final):
    bp, tp, _ = x_prompt.shape
    bs, ts, _ = x_sample.shape
    depth = w_in.shape[0]
    past = cache_k.shape[2]
    n_p, n_s = bp * tp, bs * ts
    n = n_p + n_s
    tm = ROW_TILE
    assert n % tm == 0 and tp % tm == 0 and n_p % ts == 0 and tp % KEY_BLOCK == 0
    assert ts <= CHUNK and past % CHUNK == 0 and ts % 16 == 0
    topk_p = min(TOPK_MAX, tp // 4)
    topk_s = min(TOPK_MAX, (past + ts) // 4)
    ks_rows = -(-(past + ts) // KEY_BLOCK) * KEY_BLOCK
    shift = max(1, (n - 1).bit_length())
    assert N_EXPERTS << shift <= 1 << 24
    n_blocks = -(-(n * TOP_K) // EXPERT_ROWS) + N_EXPERTS

    pos_p = jnp.arange(tp, dtype=I32)
    pos_s = past + jnp.arange(ts, dtype=I32)
    pos_all = jnp.concatenate([jnp.tile(pos_p, bp), jnp.tile(pos_s, bs)])
    cos_t, sin_s = _rope_tables(pos_all)

    x_all = jnp.concatenate([x_prompt.reshape(n_p, D_MODEL), x_sample.reshape(n_s, D_MODEL)], axis=0)
    outs = {k: [] for k in ("nk_p", "nv_p", "nki_p", "nmk_p", "nmv_p", "ns_p", "nk_s", "nv_s", "nki_s", "ns_s")}

    for l in range(depth):
        a = _inproj(x_all, cos_t, sin_s, g_mix_norm[l], w_in[l], b_gate[l], w_gla_rank[l], b_gla_rank[l])

        n_qt = tp // tm
        oa_p = _dsa(a["qi"], a["wi"], a["qa"], a["kipad"], a["kpad"], a["vpad"], batch=bp, tq=tm, n_q_tiles=n_qt,
                    q_block0=0, k_rows=tp, topk=topk_p, qpos_fn=lambda j: j * tm,
                    nkb_fn=lambda j: (j + 1) * (tm // KEY_BLOCK), l_valid=tp, name="dsa_prompt")
        kipad_s, kpad_s, vpad_s = _key_side_sample(cache_k[l], cache_v[l], cache_kidx[l], a["kpad"][n_p:],
                                                   a["vpad"][n_p:], a["kipad"][n_p:], ks_rows)
        oa_s = _dsa(a["qi"], a["wi"], a["qa"], kipad_s, kpad_s, vpad_s, batch=bs, tq=ts, n_q_tiles=1,
                    q_block0=n_p // ts, k_rows=ks_rows, topk=topk_s, qpos_fn=lambda j: past,
                    nkb_fn=lambda j: ks_rows // KEY_BLOCK, l_valid=past + ts, name="dsa_sample")

        s_zero = jnp.zeros((bp, N_HEADS_G, DK_G, DV_G), F32)
        ob_p, s_p = _gla(a["qg"], a["kg"], a["lg"], a["vg"], a["og"], g_gla_norm[l], s_zero, batch=bp, tt=tm,
                         n_tiles=tp // tm, block0=0, chunk=CHUNK, name="gla_prompt")
        ob_s, s_s = _gla(a["qg"], a["kg"], a["lg"], a["vg"], a["og"], g_gla_norm[l], state_gla[l], batch=bs, tt=ts,
                         n_tiles=1, block0=n_p // ts, chunk=ts, name="gla_sample")

        mk, mv = _memkv(mem_prompt.reshape(bp * N_MEM, D_MODEL), g_mem_norm[l], w_mem_kv[l])
        om_p = _memattn(a["qm"], mk, mv, batch=bp, tq=tm, n_tiles=tp // tm, block0=0, name="mem_prompt")
        om_s = _memattn(a["qm"], cache_mem_k[l].reshape(bs * N_MEM, W_M), cache_mem_v[l].reshape(bs * N_MEM, W_M),
                        batch=bs, tq=ts, n_tiles=1, block0=n_p // ts, name="mem_sample")

        oa = jnp.concatenate([oa_p, oa_s], axis=0)
        ob = jnp.concatenate([ob_p, ob_s], axis=0)
        om = jnp.concatenate([om_p, om_s], axis=0)
        x2, hn, rgate, rcode, cnt = _merge(x_all, oa, ob, om, a["gates"], w_br_a[l], w_br_b[l], w_br_m[l],
                                                 w_o[l], g_ffn_norm[l], w_router[l], b_router[l], shift=shift)

        counts = cnt[0, :N_EXPERTS].astype(I32)
        nblk = (counts + EXPERT_ROWS - 1) // EXPERT_ROWS
        cum = jnp.cumsum(nblk)
        pstart = ((cum - nblk) * EXPERT_ROWS).astype(I32)
        n_used = cum[-1].reshape(1).astype(I32)
        be = jnp.minimum(jnp.searchsorted(cum, jnp.arange(n_blocks, dtype=I32), side="right").astype(I32),
                         N_EXPERTS - 1)
        code = rcode[:, :TOP_K]

        xs = _dispatch(pstart, counts, n_used, code, hn, shift=shift, n_blocks=n_blocks)
        ys = _experts(be, n_used, xs, w_gu[l], b_gu[l], w_dn[l], b_dn[l], n_blocks=n_blocks)
        x_all = _combine(pstart, code, x2, rgate, g_final, ys, shift=shift, final_norm=(l == depth - 1))

        outs["nk_p"].append(a["ka"][:n_p].reshape(bp, tp, N_KV_A, HEAD_DIM_A))
        outs["nv_p"].append(a["va"][:n_p].reshape(bp, tp, N_KV_A, HEAD_DIM_A))
        outs["nki_p"].append(a["ki"][:n_p].reshape(bp, tp, IDX_DIM))
        outs["nmk_p"].append(mk.reshape(bp, N_MEM, N_HEADS_M, HEAD_DIM_M))
        outs["nmv_p"].append(mv.reshape(bp, N_MEM, N_HEADS_M, HEAD_DIM_M))
        outs["ns_p"].append(s_p)
        outs["nk_s"].append(a["ka"][n_p:].reshape(bs, ts, N_KV_A, HEAD_DIM_A))
        outs["nv_s"].append(a["va"][n_p:].reshape(bs, ts, N_KV_A, HEAD_DIM_A))
        outs["nki_s"].append(a["ki"][n_p:].reshape(bs, ts, IDX_DIM))
        outs["ns_s"].append(s_s)

    y_prompt = x_all[:n_p].reshape(bp, tp, D_MODEL)
    y_sample = x_all[n_p:].reshape(bs, ts, D_MODEL)
    st = lambda k: jnp.stack(outs[k])
    return (y_prompt, y_sample, st("nk_p"), st("nv_p"), st("nki_p"), st("nmk_p"), st("nmv_p"), st("ns_p"),
            st("nk_s"), st("nv_s"), st("nki_s"), st("ns_s"))
```

```python
import functools

import jax
import jax.numpy as jnp
from jax import lax
from jax.experimental import pallas as pl
from jax.experimental.pallas import tpu as pltpu

D_MODEL = 1024
CHUNK = 64
ROPE_THETA = 10000.0
EPS = 1e-6
N_HEADS_A = 8
N_KV_A = 2
HEAD_DIM_A = 64
N_HEADS_IDX = 8
IDX_DIM = 64
TOPK_MAX = 256
N_HEADS_G = 4
DK_G = 64
DV_G = 128
GATE_RANK = 16
GATE_TAU = 16.0
N_MEM = 256
N_HEADS_M = 4
HEAD_DIM_M = 128
N_EXPERTS = 32
TOP_K = 4
SWIGLU_LIMIT = 7.0
SWIGLU_ALPHA = 1.702
N_BRANCH = 3

W_A = N_HEADS_A * HEAD_DIM_A
W_KVA = N_KV_A * HEAD_DIM_A
W_QI = N_HEADS_IDX * IDX_DIM
W_KG = N_HEADS_G * DK_G
W_VG = N_HEADS_G * DV_G
W_M = N_HEADS_M * HEAD_DIM_M
IN_WIDTHS = (W_A, W_KVA, W_KVA, W_QI, IDX_DIM, N_HEADS_IDX, W_KG, W_KG, W_VG, GATE_RANK, W_VG, W_M,
             N_BRANCH * D_MODEL)

LANES = 128
ROW_TILE = 256
KEY_BLOCK = 256
EXPERT_ROWS = 512
VMEM_LIMIT = 56 * 1024 * 1024
INT_MIN = -2147483648

F32 = jnp.float32
BF16 = jnp.bfloat16
I32 = jnp.int32


def _cparams(sem):
    return pltpu.CompilerParams(dimension_semantics=sem, vmem_limit_bytes=VMEM_LIMIT)


def _dot(a, b):
    return jnp.dot(a, b, preferred_element_type=F32)


def _dot_nt(a, b):
    return lax.dot_general(a, b, (((1,), (1,)), ((), ())), preferred_element_type=F32)


def _lane_iota(shape):
    return lax.broadcasted_iota(I32, shape, len(shape) - 1)


def _row_iota(shape):
    return lax.broadcasted_iota(I32, shape, len(shape) - 2)


def _rope_slab(z, cos_t, sin_s):
    w = z.shape[1]
    reps = w // LANES
    cos_w = jnp.concatenate([cos_t] * reps, axis=1) if reps > 1 else cos_t
    sin_w = jnp.concatenate([sin_s] * reps, axis=1) if reps > 1 else sin_s
    first = (_lane_iota(z.shape) % HEAD_DIM_A) < (HEAD_DIM_A // 2)
    nxt = pltpu.roll(z, w - HEAD_DIM_A // 2, axis=1)
    prv = pltpu.roll(z, HEAD_DIM_A // 2, axis=1)
    return z * cos_w + jnp.where(first, nxt, prv) * sin_w


def _inproj_kernel(xp_ref, xs_ref, g_ref, cosp_ref, sinp_ref, coss_ref, sins_ref, wrope_ref, wplain_ref,
                   wsmall_ref, wgate_ref, bgate_ref, wrank_ref, brank_ref,
                   qa_ref, qi_ref, ka_ref, kpad_ref, ki_ref, kipad_ref, va_ref, vpad_ref, wi_ref,
                   qg_ref, kg_ref, vg_ref, lg_ref, og_ref, qm_ref, gates_ref, h_ref, *, n_prompt_tiles):
    is_p = pl.program_id(0) < n_prompt_tiles
    x = jnp.where(is_p, xp_ref[...], xs_ref[...])
    y = x * lax.rsqrt(jnp.mean(x * x, axis=-1, keepdims=True) + EPS)
    h_ref[...] = (y * g_ref[...]).astype(BF16)
    cos_t = jnp.where(is_p, cosp_ref[...], coss_ref[...])
    sin_s = jnp.where(is_p, sinp_ref[...], sins_ref[...])
    lo = _lane_iota((x.shape[0], LANES)) < HEAD_DIM_A

    qa = _rope_slab(_dot(h_ref[...], wrope_ref[:, 0:512]), cos_t, sin_s)
    qa_ref[...] = (qa * (HEAD_DIM_A ** -0.5)).astype(BF16)
    qi = _rope_slab(_dot(h_ref[...], wrope_ref[:, 512:1024]), cos_t, sin_s)
    qi_ref[...] = qi.astype(BF16)
    kk = _rope_slab(_dot(h_ref[...], wrope_ref[:, 1024:1280]), cos_t, sin_s)
    ka = kk[:, 0:LANES]
    ka_ref[...] = ka
    ka_sw = pltpu.roll(ka, HEAD_DIM_A, axis=1)
    zero = jnp.zeros_like(ka)
    kpad_ref[:, 0:128] = jnp.where(lo, ka, zero).astype(BF16)
    kpad_ref[:, 128:256] = jnp.where(lo, zero, ka_sw).astype(BF16)
    kpad_ref[:, 256:384] = jnp.where(lo, ka_sw, zero).astype(BF16)
    kpad_ref[:, 384:512] = jnp.where(lo, zero, ka).astype(BF16)
    kis = kk[:, LANES:2 * LANES]
    ki_ref[...] = kis[:, 0:IDX_DIM]
    kipad_ref[:, 0:128] = kis.astype(BF16)
    kipad_ref[:, 128:256] = pltpu.roll(kis, IDX_DIM, axis=1).astype(BF16)

    vq = _dot(h_ref[...], wplain_ref[:, 0:640])
    va = vq[:, 0:128]
    va_ref[...] = va
    vpad_ref[:, 0:128] = va.astype(BF16)
    vpad_ref[:, 128:256] = pltpu.roll(va, HEAD_DIM_A, axis=1).astype(BF16)
    qg_ref[...] = vq[:, 128:384] * (DK_G ** -0.5)
    kg_ref[...] = vq[:, 384:640]
    vg_ref[...] = _dot(h_ref[...], wplain_ref[:, 640:1152])
    og_ref[...] = _dot(h_ref[...], wplain_ref[:, 1152:1664])
    qm_ref[...] = _dot(h_ref[...], wplain_ref[:, 1664:2176]).astype(BF16)

    sm = _dot(h_ref[...], wsmall_ref[...])
    wi_ref[...] = sm[:, 0:128] * (N_HEADS_IDX ** -0.5 * IDX_DIM ** -0.5)
    pre = _dot(sm[:, 128:256].astype(BF16), wrank_ref[...]) + brank_ref[...]
    log_sig = jnp.minimum(pre, 0.0) - jnp.log(1.0 + jnp.exp(-jnp.abs(pre)))
    lg_ref[...] = log_sig / GATE_TAU

    for c in range(N_BRANCH * D_MODEL // 512):
        gt = _dot(h_ref[...], wgate_ref[:, c * 512:(c + 1) * 512]) + bgate_ref[:, c * 512:(c + 1) * 512]
        gates_ref[:, c * 512:(c + 1) * 512] = (1.0 / (1.0 + jnp.exp(-gt))).astype(BF16)


def _split_w_in(w_in):
    cuts, acc = [], 0
    for w in IN_WIDTHS:
        cuts.append((acc, acc + w))
        acc += w
    return [w_in[:, a:b] for a, b in cuts]


def _two_group_maps(n_prompt_tiles):
    pmap = lambda i, *_: (jnp.minimum(i, n_prompt_tiles - 1), 0)
    smap = lambda i, *_: (jnp.maximum(i - n_prompt_tiles, 0), 0)
    return pmap, smap


def _inproj(xp, xs, tabs, g_mix, w_in, b_gate, w_rank, b_rank, *, tiles_per_seq):
    n = xp.shape[0] + xs.shape[0]
    qa, ka, va, qi, ki, wi, qg, kg, vg, ag, og, qm, gt = _split_w_in(w_in)
    zc = lambda k: jnp.zeros((D_MODEL, k), F32)
    wrope = jnp.concatenate([qa, qi, ka, ki, zc(LANES - IDX_DIM)], axis=1).astype(BF16)
    wplain = jnp.concatenate([va, qg, kg, vg, og, qm], axis=1).astype(BF16)
    wsmall = jnp.concatenate([wi, zc(LANES - N_HEADS_IDX), ag, zc(LANES - GATE_RANK)], axis=1).astype(BF16)
    wgate = gt.astype(BF16)
    wrank = jnp.concatenate([w_rank, jnp.zeros((LANES - GATE_RANK, W_KG), F32)], axis=0).astype(BF16)

    tm = ROW_TILE
    row = lambda w: pl.BlockSpec((tm, w), lambda i: (i, 0))
    full = lambda a: pl.BlockSpec(a.shape, lambda i: (0, 0))
    outs = [("qa", 512, BF16), ("qi", 512, BF16), ("ka", 128, F32), ("kpad", 512, BF16), ("ki", 64, F32),
            ("kipad", 256, BF16), ("va", 128, F32), ("vpad", 256, BF16), ("wi", 128, F32), ("qg", 256, F32),
            ("kg", 256, F32), ("vg", 512, F32), ("lg", 256, F32), ("og", 512, F32), ("qm", 512, BF16),
            ("gates", N_BRANCH * D_MODEL, BF16)]
    g2 = g_mix.reshape(1, D_MODEL)
    bg2 = b_gate.reshape(1, -1)
    br2 = b_rank.reshape(1, -1)
    npt = xp.shape[0] // tm
    pmap, smap = _two_group_maps(npt)
    tab_p = pl.BlockSpec((tm, LANES), lambda i: (jnp.minimum(i, npt - 1) % tiles_per_seq, 0))
    tab_s = pl.BlockSpec((tm, LANES), lambda i: (0, 0))
    res = pl.pallas_call(
        functools.partial(_inproj_kernel, n_prompt_tiles=npt),
        grid=(n // tm,),
        in_specs=[pl.BlockSpec((tm, D_MODEL), pmap), pl.BlockSpec((tm, D_MODEL), smap), full(g2), tab_p, tab_p,
                  tab_s, tab_s, full(wrope), full(wplain), full(wsmall), full(wgate), full(bg2), full(wrank),
                  full(br2)],
        out_specs=[row(w) for _, w, _ in outs],
        out_shape=[jax.ShapeDtypeStruct((n, w), dt) for _, w, dt in outs],
        scratch_shapes=[pltpu.VMEM((tm, D_MODEL), BF16)],
        compiler_params=_cparams(("parallel",)),
        name="inproj",
    )(xp, xs, g2, *tabs, wrope, wplain, wsmall, wgate, bg2, wrank, br2)
    return {name: r for (name, _, _), r in zip(outs, res)}


def _rope_tables(pos):
    half = HEAD_DIM_A // 2
    inv_freq = ROPE_THETA ** (-jnp.arange(half, dtype=F32) / half)
    ang = pos.astype(F32)[:, None] * inv_freq[None, :]
    cos = jnp.cos(ang)
    sin = jnp.sin(ang)
    cos_t = jnp.concatenate([cos, cos, cos, cos], axis=1)
    sin_s = jnp.concatenate([-sin, sin, -sin, sin], axis=1)
    return cos_t, sin_s


def _dsa_kernel(qi_ref, wi_ref, qa_ref, kipad_ref, kpad_ref, vpad_ref, o_ref,
                key_ref, bias_ref, l_ref, wb_ref, m_ref, s_ref, acc_ref, *, tq, nkb_max, topk, qpos_fn, nkb_fn,
                l_valid):
    j = pl.program_id(1)
    nkb = nkb_fn(j)
    qpos0 = qpos_fn(j)
    kb_w = KEY_BLOCK

    for hh in range(N_HEADS_IDX):
        wcol = jnp.broadcast_to(wi_ref[:, hh:hh + 1], (tq, LANES))
        wb_ref[hh] = jnp.concatenate([wcol, wcol], axis=1)

    qchunk = (qpos0 + _row_iota((tq, kb_w))) // CHUNK

    def score_body(kb, carry):
        k0 = pl.multiple_of(kb * kb_w, kb_w)
        ka_blk = kipad_ref[pl.ds(k0, kb_w), 0:128]
        kb_blk = kipad_ref[pl.ds(k0, kb_w), 128:256]
        s = jnp.zeros((tq, kb_w), F32)
        for p in range(N_HEADS_IDX // 2):
            lhs = qi_ref[:, p * LANES:(p + 1) * LANES]
            s = s + jnp.maximum(_dot_nt(lhs, ka_blk), 0.0) * wb_ref[2 * p]
            s = s + jnp.maximum(_dot_nt(lhs, kb_blk), 0.0) * wb_ref[2 * p + 1]
        bits = lax.bitcast_convert_type(s, I32)
        key = bits ^ ((bits >> 31) & 0x7FFFFFFF)
        kidx = k0 + _lane_iota((tq, kb_w))
        adm = jnp.logical_and(kidx // CHUNK <= qchunk, kidx < l_valid)
        key_ref[kb] = jnp.where(adm, key, INT_MIN)
        return carry

    lax.fori_loop(0, nkb, score_body, 0)

    def count(pred_fn):
        def body(kb, acc):
            blk = key_ref[kb]
            kidx = kb * kb_w + _lane_iota((tq, kb_w))
            c = jnp.where(pred_fn(blk, kidx), 1.0, 0.0)
            return acc + c[:, 0:LANES] + c[:, LANES:2 * LANES]
        acc = lax.fori_loop(0, nkb, body, jnp.zeros((tq, LANES), F32))
        return jnp.sum(acc, axis=1, keepdims=True)

    def wide(v):
        return jnp.concatenate([v, v], axis=1)

    kf = float(topk)

    def search():
        c0 = count(lambda blk, kidx: blk >= 0)
        thr0 = jnp.where(jnp.broadcast_to(c0, (tq, LANES)) >= kf, 0, INT_MIN).astype(I32)

        def bit_body(i, thr):
            cand = thr + jnp.left_shift(jnp.int32(1), 30 - i)
            cand_w = wide(cand)
            c = count(lambda blk, kidx: blk >= cand_w)
            return jnp.where(jnp.broadcast_to(c, (tq, LANES)) >= kf, cand, thr)

        return lax.fori_loop(0, 31, bit_body, thr0)

    thr = lax.cond(nkb * kb_w > topk, search, lambda: jnp.full((tq, LANES), INT_MIN, I32))
    thr_w = wide(thr)

    n_gt = count(lambda blk, kidx: blk > thr_w)
    n_ge = count(lambda blk, kidx: blk >= thr_w)
    need = jnp.broadcast_to(kf - n_gt, (tq, LANES))
    idx_cap = nkb_max * kb_w

    def tie_search():
        nbits = max(1, (idx_cap - 1).bit_length())

        def bit_body(i, c):
            cand = c + jnp.left_shift(jnp.int32(1), nbits - 1 - i)
            cand_w = wide(cand)
            cnt = count(lambda blk, kidx: jnp.logical_and(blk == thr_w, kidx < cand_w))
            return jnp.where(jnp.broadcast_to(cnt, (tq, LANES)) <= need, cand, c)

        return lax.fori_loop(0, nbits, bit_body, jnp.zeros((tq, LANES), I32))

    has_ties = jnp.max(n_ge) > kf
    cut_w = wide(lax.cond(has_ties, tie_search, lambda: jnp.full((tq, LANES), idx_cap, I32)))

    def bias_body(kb, carry):
        blk = key_ref[kb]
        kidx = kb * kb_w + _lane_iota((tq, kb_w))
        sel = jnp.logical_or(blk > thr_w, jnp.logical_and(blk == thr_w, kidx < cut_w))
        sel = jnp.logical_and(sel, blk != INT_MIN)
        bias_ref[kb] = jnp.where(sel, 0.0, -jnp.inf)
        return carry

    lax.fori_loop(0, nkb, bias_body, 0)

    heads = []
    for p in range(N_HEADS_A // 2):
        n = (2 * p) // (N_HEADS_A // N_KV_A)
        for e in range(2):
            heads.append((p, (2 * n + e) * LANES, LANES if n != e else 0))

    m_ref[...] = jnp.full(m_ref.shape, -jnp.inf, F32)
    s_ref[...] = jnp.zeros(s_ref.shape, F32)
    acc_ref[...] = jnp.zeros(acc_ref.shape, F32)

    def logit_body(kb, carry):
        k0 = pl.multiple_of(kb * kb_w, kb_w)
        bias = bias_ref[kb]
        for hd, (p, kcol, _) in enumerate(heads):
            lg = _dot_nt(qa_ref[:, p * LANES:(p + 1) * LANES], kpad_ref[pl.ds(k0, kb_w), kcol:kcol + LANES]) + bias
            l_ref[hd, kb] = lg
            m_ref[hd] = jnp.maximum(m_ref[hd], jnp.maximum(lg[:, 0:LANES], lg[:, LANES:2 * LANES]))
        return carry

    lax.fori_loop(0, nkb, logit_body, 0)

    for hd in range(N_HEADS_A):
        m_ref[hd] = jnp.broadcast_to(jnp.max(m_ref[hd], axis=1, keepdims=True), (tq, LANES))

    def pv_body(kb, carry):
        k0 = pl.multiple_of(kb * kb_w, kb_w)
        for hd, (_, _, vcol) in enumerate(heads):
            ex = jnp.exp(l_ref[hd, kb] - wide(m_ref[hd]))
            s_ref[hd] = s_ref[hd] + ex[:, 0:LANES] + ex[:, LANES:2 * LANES]
            acc_ref[hd] = acc_ref[hd] + _dot(ex.astype(BF16), vpad_ref[pl.ds(k0, kb_w), vcol:vcol + LANES])
        return carry

    lax.fori_loop(0, nkb, pv_body, 0)

    lo = _lane_iota((tq, LANES)) < HEAD_DIM_A
    for p in range(N_HEADS_A // 2):
        halves = [acc_ref[2 * p + e] / jnp.sum(s_ref[2 * p + e], axis=1, keepdims=True) for e in range(2)]
        o_ref[:, p * LANES:(p + 1) * LANES] = jnp.where(lo, halves[0], halves[1]).astype(BF16)


def _dsa(qi, wi, qa, kipad, kpad, vpad, *, batch, tq, n_q_tiles, q_block0, k_rows, topk, qpos_fn, nkb_fn,
         l_valid, name):
    nkb_max = k_rows // KEY_BLOCK
    qmap = lambda b, j: (q_block0 + b * n_q_tiles + j, 0)
    kmap = lambda b, j: (b, 0)
    kern = functools.partial(_dsa_kernel, tq=tq, nkb_max=nkb_max, topk=topk, qpos_fn=qpos_fn, nkb_fn=nkb_fn,
                             l_valid=l_valid)
    return pl.pallas_call(
        kern,
        grid=(batch, n_q_tiles),
        in_specs=[pl.BlockSpec((tq, W_QI), qmap), pl.BlockSpec((tq, LANES), qmap), pl.BlockSpec((tq, W_A), qmap),
                  pl.BlockSpec((k_rows, 256), kmap), pl.BlockSpec((k_rows, 512), kmap),
                  pl.BlockSpec((k_rows, 256), kmap)],
        out_specs=pl.BlockSpec((tq, W_A), lambda b, j: (b * n_q_tiles + j, 0)),
        out_shape=jax.ShapeDtypeStruct((batch * n_q_tiles * tq, W_A), BF16),
        scratch_shapes=[pltpu.VMEM((nkb_max, tq, KEY_BLOCK), I32),
                        pltpu.VMEM((nkb_max, tq, KEY_BLOCK), F32),
                        pltpu.VMEM((N_HEADS_A, nkb_max, tq, KEY_BLOCK), F32),
                        pltpu.VMEM((N_HEADS_IDX, tq, KEY_BLOCK), F32),
                        pltpu.VMEM((N_HEADS_A, tq, LANES), F32),
                        pltpu.VMEM((N_HEADS_A, tq, LANES), F32),
                        pltpu.VMEM((N_HEADS_A, tq, LANES), F32)],
        compiler_params=_cparams(("parallel", "arbitrary")),
        name=name,
    )(qi, wi, qa, kipad, kpad, vpad)


def _cumsum_rows(x):
    c = x.shape[0]
    rows = _row_iota(x.shape)
    s = 1
    while s < c:
        x = x + jnp.where(rows >= s, pltpu.roll(x, s, axis=0), 0.0)
        s *= 2
    return x


def _gla_kernel(q_ref, k_ref, lg_ref, v_ref, og_ref, gn_ref, s0_ref, o_ref, sout_ref, st_ref, *, tt, chunk):
    t = pl.program_id(1)
    n_pairs = N_HEADS_G // 2

    @pl.when(t == 0)
    def _():
        for p in range(n_pairs):
            pair = jnp.concatenate([s0_ref[2 * p], s0_ref[2 * p + 1]], axis=0)
            st_ref[p] = pair.T

    lane = _lane_iota((chunk, LANES))
    masks = (lane < DK_G, lane >= DK_G)
    causal = _row_iota((chunk, chunk)) >= _lane_iota((chunk, chunk))
    pad_rows = LANES - chunk

    for c in range(tt // chunk):
        r0 = c * chunk
        bcum = _cumsum_rows(lg_ref[r0:r0 + chunk, :])
        for p in range(n_pairs):
            sl = slice(p * LANES, (p + 1) * LANES)
            b = bcum[:, sl]
            q = q_ref[r0:r0 + chunk, sl]
            k = k_ref[r0:r0 + chunk, sl]
            b_last = b[chunk - 1:chunk, :]
            b_mid = b[chunk // 2:chunk // 2 + 1, :]
            qb = q * jnp.exp(b)
            qe = (q * jnp.exp(b - b_mid)).astype(BF16)
            ke = k * jnp.exp(b_mid - b)
            kd = k * jnp.exp(b_last - b)
            st_old = st_ref[p]
            st_bf = st_old.astype(BF16)
            upd = jnp.zeros((LANES, LANES), F32)
            for e in range(2):
                hd = 2 * p + e
                vh = v_ref[r0:r0 + chunk, hd * DV_G:(hd + 1) * DV_G]
                a = _dot_nt(qe, jnp.where(masks[e], ke, 0.0).astype(BF16))
                a = jnp.where(causal, a, 0.0)
                o = _dot(a.astype(BF16), vh.astype(BF16))
                o = o + _dot_nt(jnp.where(masks[e], qb, 0.0).astype(BF16), st_bf)
                vh_sq = jnp.concatenate([vh, jnp.zeros((pad_rows, DV_G), F32)], axis=0) if pad_rows else vh
                kd_m = jnp.where(masks[e], kd, 0.0)
                kd_sq = jnp.concatenate([kd_m, jnp.zeros((pad_rows, LANES), F32)], axis=0) if pad_rows else kd_m
                upd = upd + _dot(vh_sq.T.astype(BF16), kd_sq.astype(BF16))
                on = o * lax.rsqrt(jnp.mean(o * o, axis=-1, keepdims=True) + EPS)
                on = on * gn_ref[:, hd * DV_G:(hd + 1) * DV_G]
                og = og_ref[r0:r0 + chunk, hd * DV_G:(hd + 1) * DV_G]
                o_ref[r0:r0 + chunk, hd * DV_G:(hd + 1) * DV_G] = (on * (og / (1.0 + jnp.exp(-og)))).astype(BF16)
            st_ref[p] = st_old * jnp.exp(b_last) + upd

    @pl.when(t == pl.num_programs(1) - 1)
    def _():
        for p in range(n_pairs):
            pair = st_ref[p].T
            sout_ref[2 * p] = pair[0:DK_G, :]
            sout_ref[2 * p + 1] = pair[DK_G:2 * DK_G, :]


def _gla(qg, kg, lg, vg, og, g_norm, s0, *, batch, tt, n_tiles, block0, chunk, name):
    rmap = lambda b, t: (block0 + b * n_tiles + t, 0)
    smap = lambda b, t: (b, 0, 0, 0)
    gn2 = g_norm.reshape(1, W_VG)
    kern = functools.partial(_gla_kernel, tt=tt, chunk=chunk)
    return pl.pallas_call(
        kern,
        grid=(batch, n_tiles),
        in_specs=[pl.BlockSpec((tt, W_KG), rmap), pl.BlockSpec((tt, W_KG), rmap), pl.BlockSpec((tt, W_KG), rmap),
                  pl.BlockSpec((tt, W_VG), rmap), pl.BlockSpec((tt, W_VG), rmap),
                  pl.BlockSpec((1, W_VG), lambda b, t: (0, 0)),
                  pl.BlockSpec((None, N_HEADS_G, DK_G, DV_G), smap)],
        out_specs=[pl.BlockSpec((tt, W_VG), lambda b, t: (b * n_tiles + t, 0)),
                   pl.BlockSpec((None, N_HEADS_G, DK_G, DV_G), smap)],
        out_shape=[jax.ShapeDtypeStruct((batch * n_tiles * tt, W_VG), BF16),
                   jax.ShapeDtypeStruct((batch, N_HEADS_G, DK_G, DV_G), F32)],
        scratch_shapes=[pltpu.VMEM((N_HEADS_G // 2, LANES, LANES), F32)],
        compiler_params=_cparams(("parallel", "arbitrary")),
        name=name,
    )(qg, kg, lg, vg, og, gn2, s0)


def _memkv_kernel(m_ref, g_ref, w_ref, mk_ref, mv_ref):
    x = m_ref[...]
    y = x * lax.rsqrt(jnp.mean(x * x, axis=-1, keepdims=True) + EPS)
    h = (y * g_ref[...]).astype(BF16)
    mk_ref[...] = _dot(h, w_ref[:, 0:W_M])
    mv_ref[...] = _dot(h, w_ref[:, W_M:2 * W_M])


def _memkv(mem_rows, g_mem, w_mem_kv):
    n = mem_rows.shape[0]
    tm = ROW_TILE
    g2 = g_mem.reshape(1, D_MODEL)
    wb = w_mem_kv.astype(BF16)
    return pl.pallas_call(
        _memkv_kernel,
        grid=(n // tm,),
        in_specs=[pl.BlockSpec((tm, D_MODEL), lambda i: (i, 0)), pl.BlockSpec((1, D_MODEL), lambda i: (0, 0)),
                  pl.BlockSpec(wb.shape, lambda i: (0, 0))],
        out_specs=[pl.BlockSpec((tm, W_M), lambda i: (i, 0))] * 2,
        out_shape=[jax.ShapeDtypeStruct((n, W_M), F32)] * 2,
        compiler_params=_cparams(("parallel",)),
        name="memkv",
    )(mem_rows, g2, wb)


def _memattn_kernel(q_ref, mk_ref, mv_ref, o_ref):
    for hd in range(N_HEADS_M):
        sl = slice(hd * HEAD_DIM_M, (hd + 1) * HEAD_DIM_M)
        logits = _dot_nt(q_ref[:, sl], mk_ref[:, sl].astype(BF16)) * (HEAD_DIM_M ** -0.5)
        m = jnp.max(logits, axis=-1, keepdims=True)
        ex = jnp.exp(logits - m)
        o = _dot(ex.astype(BF16), mv_ref[:, sl].astype(BF16))
        o_ref[:, sl] = (o / jnp.sum(ex, axis=-1, keepdims=True)).astype(BF16)


def _memattn(qm, mk, mv, *, batch, tq, n_tiles, block0, name):
    qmap = lambda b, t: (block0 + b * n_tiles + t, 0)
    kmap = lambda b, t: (b, 0)
    return pl.pallas_call(
        _memattn_kernel,
        grid=(batch, n_tiles),
        in_specs=[pl.BlockSpec((tq, W_M), qmap), pl.BlockSpec((N_MEM, W_M), kmap),
                  pl.BlockSpec((N_MEM, W_M), kmap)],
        out_specs=pl.BlockSpec((tq, W_M), lambda b, t: (b * n_tiles + t, 0)),
        out_shape=jax.ShapeDtypeStruct((batch * n_tiles * tq, W_M), BF16),
        compiler_params=_cparams(("parallel", "arbitrary")),
        name=name,
    )(qm, mk, mv)


def _merge_kernel(xp_ref, xs_ref, oap_ref, oas_ref, obp_ref, obs_ref, omp_ref, oms_ref, gates_ref, wa_ref, wb_ref,
                  wm_ref, wo_ref, gf_ref, wr_ref, br_ref, x2_ref, hn_ref, rgate_ref, rcode_ref, cnt_ref, run_ref, *,
                  shift, n_prompt_tiles):
    i = pl.program_id(0)
    tm = xp_ref.shape[0]
    is_p = i < n_prompt_tiles
    pick = lambda p_ref, s_ref: jnp.where(is_p, p_ref[...], s_ref[...])

    @pl.when(i == 0)
    def _():
        run_ref[...] = jnp.zeros_like(run_ref)

    d = D_MODEL
    m = gates_ref[:, 0:d].astype(F32) * _dot(pick(oap_ref, oas_ref), wa_ref[...])
    m = m + gates_ref[:, d:2 * d].astype(F32) * _dot(pick(obp_ref, obs_ref), wb_ref[...])
    m = m + gates_ref[:, 2 * d:3 * d].astype(F32) * _dot(pick(omp_ref, oms_ref), wm_ref[...])
    x2 = pick(xp_ref, xs_ref) + _dot(m.astype(BF16), wo_ref[...])
    x2_ref[...] = x2
    hn = x2 * lax.rsqrt(jnp.mean(x2 * x2, axis=-1, keepdims=True) + EPS) * gf_ref[...]
    hn_ref[...] = hn

    lane = _lane_iota((tm, LANES))
    lane_f = lane.astype(F32)
    logits = _dot(hn.astype(BF16), wr_ref[...]) + br_ref[...]
    work = jnp.where(lane < N_EXPERTS, logits, -jnp.inf)
    vals, hots, eidx = [], [], []
    for _ in range(TOP_K):
        mx = jnp.max(work, axis=-1, keepdims=True)
        idx = jnp.min(jnp.where(work == mx, lane_f, float(LANES)), axis=-1, keepdims=True)
        hot = lane_f == idx
        vals.append(mx)
        hots.append(hot)
        eidx.append(idx)
        work = jnp.where(hot, -jnp.inf, work)
    exps = [jnp.exp(v - vals[0]) for v in vals]
    den = exps[0] + exps[1] + exps[2] + exps[3]

    multi = jnp.where(jnp.logical_or(jnp.logical_or(hots[0], hots[1]), jnp.logical_or(hots[2], hots[3])), 1.0, 0.0)
    tri = jnp.where(_row_iota((tm, tm)) > _lane_iota((tm, tm)), 1.0, 0.0).astype(BF16)
    before = _dot(tri, multi.astype(BF16)) + run_ref[...]
    run_ref[...] = run_ref[...] + jnp.sum(multi, axis=0, keepdims=True)
    cnt_ref[...] = run_ref[...]

    rgate = jnp.zeros((tm, LANES), F32)
    rcode = jnp.zeros((tm, LANES), F32)
    for jj in range(TOP_K):
        rank = jnp.sum(jnp.where(hots[jj], before, 0.0), axis=-1, keepdims=True)
        sel = lane == jj
        rgate = jnp.where(sel, exps[jj] / den, rgate)
        rcode = jnp.where(sel, eidx[jj] * float(1 << shift) + rank, rcode)
    rgate_ref[...] = rgate
    rcode_ref[...] = rcode.astype(I32)


def _merge(x, oa, ob, om, gates, w_br_a, w_br_b, w_br_m, w_o, g_ffn, w_router, b_router, *, shift):
    n = gates.shape[0]
    tm = ROW_TILE
    npt = x[0].shape[0] // tm
    pmap, smap = _two_group_maps(npt)
    pair = lambda w: [pl.BlockSpec((tm, w), pmap), pl.BlockSpec((tm, w), smap)]
    row = lambda w: pl.BlockSpec((tm, w), lambda i: (i, 0))
    full = lambda a: pl.BlockSpec(a.shape, lambda i: (0, 0))
    wa, wb, wm, wo = (w.astype(BF16) for w in (w_br_a, w_br_b, w_br_m, w_o))
    gf2 = g_ffn.reshape(1, D_MODEL)
    wr = jnp.concatenate([w_router, jnp.zeros((D_MODEL, LANES - N_EXPERTS), F32)], axis=1).astype(BF16)
    br = jnp.concatenate([b_router, jnp.zeros((LANES - N_EXPERTS,), F32)]).reshape(1, LANES)
    kern = functools.partial(_merge_kernel, shift=shift, n_prompt_tiles=npt)
    return pl.pallas_call(
        kern,
        grid=(n // tm,),
        in_specs=[*pair(D_MODEL), *pair(W_A), *pair(W_VG), *pair(W_M), row(N_BRANCH * D_MODEL), full(wa), full(wb),
                  full(wm), full(wo), full(gf2), full(wr), full(br)],
        out_specs=[row(D_MODEL), row(D_MODEL), row(LANES), row(LANES),
                   pl.BlockSpec((1, LANES), lambda i: (0, 0))],
        out_shape=[jax.ShapeDtypeStruct((n, D_MODEL), F32), jax.ShapeDtypeStruct((n, D_MODEL), F32),
                   jax.ShapeDtypeStruct((n, LANES), F32), jax.ShapeDtypeStruct((n, LANES), I32),
                   jax.ShapeDtypeStruct((1, LANES), F32)],
        scratch_shapes=[pltpu.VMEM((1, LANES), F32)],
        compiler_params=_cparams(("arbitrary",)),
        name="merge",
    )(*x, *oa, *ob, *om, gates, wa, wb, wm, wo, gf2, wr, br)


def _slot(code, pstart_ref, shift):
    return pstart_ref[code >> shift] + (code & ((1 << shift) - 1))


def _dispatch_kernel(pstart_ref, cnt_ref, nu_ref, code_ref, hn_ref, xs_ref, zero_ref, sem, zsem, *, shift,
                     n_blocks):
    i = pl.program_id(0)
    tm = hn_ref.shape[0]
    er = EXPERT_ROWS

    def row_copy(r, jj):
        d = _slot(code_ref[0, 0, r * TOP_K + jj], pstart_ref, shift)
        return pltpu.make_async_copy(hn_ref.at[pl.ds(r, 1), :], xs_ref.at[pl.ds(d, 1), :], sem)

    def start_body(r, carry):
        for jj in range(TOP_K):
            row_copy(r, jj).start()
        return carry

    lax.fori_loop(0, tm, start_body, 0)

    @pl.when(i == pl.num_programs(0) - 1)
    def _():
        zero_ref[...] = jnp.zeros_like(zero_ref)

        def expert_body(e, carry):
            c = cnt_ref[e]
            lo_row = pstart_ref[e] + c
            hi_row = pstart_ref[e] + (c + er - 1) // er * er

            def zcopy(r):
                return pltpu.make_async_copy(zero_ref.at[pl.ds(0, 1), :], xs_ref.at[pl.ds(r, 1), :], zsem)

            def zstart(r, cc):
                zcopy(r).start()
                return cc

            def zwait(r, cc):
                zcopy(r).wait()
                return cc

            lax.fori_loop(lo_row, hi_row, zstart, 0)
            lax.fori_loop(lo_row, hi_row, zwait, 0)
            return carry

        lax.fori_loop(0, N_EXPERTS, expert_body, 0)

        def block_body(b, carry):
            cp = pltpu.make_async_copy(zero_ref, xs_ref.at[pl.ds(b * er, er), :], zsem)
            cp.start()
            cp.wait()
            return carry

        lax.fori_loop(nu_ref[0], n_blocks, block_body, 0)

    def wait_body(r, carry):
        for jj in range(TOP_K):
            row_copy(r, jj).wait()
        return carry

    lax.fori_loop(0, tm, wait_body, 0)


def _dispatch(pstart, counts, nused, code, hn, *, shift, n_blocks):
    n = hn.shape[0]
    tm = ROW_TILE
    code3 = code.reshape(n // tm, 1, tm * TOP_K)
    kern = functools.partial(_dispatch_kernel, shift=shift, n_blocks=n_blocks)
    return pl.pallas_call(
        kern,
        grid_spec=pltpu.PrefetchScalarGridSpec(
            num_scalar_prefetch=3,
            grid=(n // tm,),
            in_specs=[pl.BlockSpec((1, 1, tm * TOP_K), lambda i, *_: (i, 0, 0), memory_space=pltpu.SMEM),
                      pl.BlockSpec((tm, D_MODEL), lambda i, *_: (i, 0))],
            out_specs=pl.BlockSpec(memory_space=pl.ANY),
            scratch_shapes=[pltpu.VMEM((EXPERT_ROWS, D_MODEL), F32), pltpu.SemaphoreType.DMA,
                            pltpu.SemaphoreType.DMA],
        ),
        out_shape=jax.ShapeDtypeStruct((n_blocks * EXPERT_ROWS, D_MODEL), F32),
        compiler_params=_cparams(("arbitrary",)),
        name="dispatch",
    )(pstart, counts, nused, code3, hn)


def _experts_kernel(be_ref, nu_ref, x_ref, wgu_ref, bgu_ref, wdn_ref, bdn_ref, y_ref, wgu_bf, wdn_bf):
    i = pl.program_id(0)
    cast_rows = 128

    @pl.when(jnp.logical_or(i == 0, be_ref[i] != be_ref[jnp.maximum(i - 1, 0)]))
    def _():
        def cast_body(c, carry):
            r0 = pl.multiple_of(c * cast_rows, cast_rows)
            wgu_bf[pl.ds(r0, cast_rows), :] = wgu_ref[pl.ds(r0, cast_rows), :].astype(BF16)
            wdn_bf[pl.ds(r0, cast_rows), :] = wdn_ref[pl.ds(r0, cast_rows), :].astype(BF16)
            return carry

        lax.fori_loop(0, D_MODEL // cast_rows, cast_body, 0)

    @pl.when(i < nu_ref[0])
    def _():
        xb = x_ref[...].astype(BF16)
        de = D_MODEL
        g = _dot(xb, wgu_bf[:, 0:de]) + bgu_ref[:, 0:de]
        u = _dot(xb, wgu_bf[:, de:2 * de]) + bgu_ref[:, de:2 * de]
        g = jnp.minimum(g, SWIGLU_LIMIT)
        u = jnp.clip(u, -SWIGLU_LIMIT, SWIGLU_LIMIT)
        a = (u + 1.0) * (g * (1.0 / (1.0 + jnp.exp(-SWIGLU_ALPHA * g))))
        y_ref[...] = _dot(a.astype(BF16), wdn_bf[...]) + bdn_ref[...]

    @pl.when(i >= nu_ref[0])
    def _():
        y_ref[...] = jnp.zeros_like(y_ref)


def _experts(be, nused, xs, wgu, b_gu, wdn, b_dn, *, n_blocks):
    bgu = b_gu.reshape(N_EXPERTS, 1, 2 * D_MODEL)
    bdn = b_dn.reshape(N_EXPERTS, 1, D_MODEL)
    er = EXPERT_ROWS
    return pl.pallas_call(
        _experts_kernel,
        grid_spec=pltpu.PrefetchScalarGridSpec(
            num_scalar_prefetch=2,
            grid=(n_blocks,),
            in_specs=[pl.BlockSpec((er, D_MODEL), lambda i, be, nu: (i, 0)),
                      pl.BlockSpec((None, D_MODEL, 2 * D_MODEL), lambda i, be, nu: (be[i], 0, 0)),
                      pl.BlockSpec((None, 1, 2 * D_MODEL), lambda i, be, nu: (be[i], 0, 0)),
                      pl.BlockSpec((None, D_MODEL, D_MODEL), lambda i, be, nu: (be[i], 0, 0)),
                      pl.BlockSpec((None, 1, D_MODEL), lambda i, be, nu: (be[i], 0, 0))],
            out_specs=pl.BlockSpec((er, D_MODEL), lambda i, be, nu: (i, 0)),
            scratch_shapes=[pltpu.VMEM((D_MODEL, 2 * D_MODEL), BF16), pltpu.VMEM((D_MODEL, D_MODEL), BF16)],
        ),
        out_shape=jax.ShapeDtypeStruct(xs.shape, F32),
        compiler_params=_cparams(("arbitrary",)),
        name="experts",
    )(be, nused, xs, wgu, bgu, wdn, bdn)


def _combine_kernel(pstart_ref, code_ref, x2_ref, gate_ref, gfin_ref, y_ref, outp_ref, outs_ref, buf_ref, sem, *,
                    shift, final_norm, n_prompt_tiles):
    tm = x2_ref.shape[0]

    def row_copy(r, jj):
        d = _slot(code_ref[0, 0, r * TOP_K + jj], pstart_ref, shift)
        return pltpu.make_async_copy(y_ref.at[pl.ds(d, 1), :], buf_ref.at[jj, pl.ds(r, 1), :], sem)

    def start_body(r, carry):
        for jj in range(TOP_K):
            row_copy(r, jj).start()
        return carry

    def wait_body(r, carry):
        for jj in range(TOP_K):
            row_copy(r, jj).wait()
        return carry

    lax.fori_loop(0, tm, start_body, 0)
    lax.fori_loop(0, tm, wait_body, 0)

    acc = x2_ref[...]
    for jj in range(TOP_K):
        acc = acc + buf_ref[jj] * gate_ref[:, jj:jj + 1]
    if final_norm:
        acc = acc * lax.rsqrt(jnp.mean(acc * acc, axis=-1, keepdims=True) + EPS) * gfin_ref[...]

    is_p = pl.program_id(0) < n_prompt_tiles

    @pl.when(is_p)
    def _():
        outp_ref[...] = acc

    @pl.when(jnp.logical_not(is_p))
    def _():
        outs_ref[...] = acc


def _combine(pstart, code, x2, gate, g_final, ys, *, shift, final_norm, n_prompt_rows):
    n = x2.shape[0]
    tm = ROW_TILE
    npt = n_prompt_rows // tm
    pmap, smap = _two_group_maps(npt)
    code3 = code.reshape(n // tm, 1, tm * TOP_K)
    gf2 = g_final.reshape(1, D_MODEL)
    return pl.pallas_call(
        functools.partial(_combine_kernel, shift=shift, final_norm=final_norm, n_prompt_tiles=npt),
        grid_spec=pltpu.PrefetchScalarGridSpec(
            num_scalar_prefetch=1,
            grid=(n // tm,),
            in_specs=[pl.BlockSpec((1, 1, tm * TOP_K), lambda i, ps: (i, 0, 0), memory_space=pltpu.SMEM),
                      pl.BlockSpec((tm, D_MODEL), lambda i, ps: (i, 0)),
                      pl.BlockSpec((tm, LANES), lambda i, ps: (i, 0)),
                      pl.BlockSpec((1, D_MODEL), lambda i, ps: (0, 0)),
                      pl.BlockSpec(memory_space=pl.ANY)],
            out_specs=[pl.BlockSpec((tm, D_MODEL), pmap), pl.BlockSpec((tm, D_MODEL), smap)],
            scratch_shapes=[pltpu.VMEM((TOP_K, tm, D_MODEL), F32), pltpu.SemaphoreType.DMA],
        ),
        out_shape=[jax.ShapeDtypeStruct((n_prompt_rows, D_MODEL), F32),
                   jax.ShapeDtypeStruct((n - n_prompt_rows, D_MODEL), F32)],
        compiler_params=_cparams(("arbitrary",)),
        name="combine",
    )(pstart, code3, x2, gate, gf2, ys)


def _key_side_sample(cache_k, cache_v, cache_kidx, kpad_new, vpad_new, kipad_new, k_rows):
    db, past = cache_k.shape[0], cache_k.shape[1]
    ck = cache_k.reshape(db, past, W_KVA)
    cv = cache_v.reshape(db, past, W_KVA)
    z64 = jnp.zeros((db, past, HEAD_DIM_A), F32)
    k0, k1 = ck[..., :HEAD_DIM_A], ck[..., HEAD_DIM_A:]
    kpad_c = jnp.concatenate([k0, z64, z64, k0, k1, z64, z64, k1], axis=-1).astype(BF16)
    vpad_c = jnp.concatenate([cv, cv[..., HEAD_DIM_A:], cv[..., :HEAD_DIM_A]], axis=-1).astype(BF16)
    kipad_c = jnp.concatenate([cache_kidx, z64, z64, cache_kidx], axis=-1).astype(BF16)
    ds = kpad_new.shape[0] // db
    pad = k_rows - past - ds

    def cat(c, new):
        w = c.shape[-1]
        parts = [c, new.reshape(db, ds, w)]
        if pad:
            parts.append(jnp.zeros((db, pad, w), BF16))
        return jnp.concatenate(parts, axis=1).reshape(db * k_rows, w)

    return cat(kipad_c, kipad_new), cat(kpad_c, kpad_new), cat(vpad_c, vpad_new)


def kernel(x_prompt, x_sample, mem_prompt, cache_k, cache_v, cache_kidx, cache_mem_k, cache_mem_v, state_gla,
           g_mix_norm, w_in, b_gate, w_gla_rank, b_gla_rank, g_gla_norm, g_mem_norm, w_mem_kv, w_br_a, w_br_b,
           w_br_m, w_o, g_ffn_norm, w_router, b_router, w_gu, b_gu, w_dn, b_dn, g_final):
    bp, tp, _ = x_prompt.shape
    bs, ts, _ = x_sample.shape
    depth = w_in.shape[0]
    past = cache_k.shape[2]
    n_p, n_s = bp * tp, bs * ts
    n = n_p + n_s
    tm = ROW_TILE
    assert n_s % tm == 0 and tp % tm == 0 and n_p % ts == 0 and tp % KEY_BLOCK == 0 and tm % ts == 0
    assert ts <= CHUNK and past % CHUNK == 0 and ts % 16 == 0
    topk_p = min(TOPK_MAX, tp // 4)
    topk_s = min(TOPK_MAX, (past + ts) // 4)
    ks_rows = -(-(past + ts) // KEY_BLOCK) * KEY_BLOCK
    shift = max(1, (n - 1).bit_length())
    assert N_EXPERTS << shift <= 1 << 24
    n_blocks = -(-(n * TOP_K) // EXPERT_ROWS) + N_EXPERTS

    tabs = (*_rope_tables(jnp.arange(tp, dtype=I32)),
            *_rope_tables(jnp.tile(past + jnp.arange(ts, dtype=I32), tm // ts)))

    x_p = x_prompt.reshape(n_p, D_MODEL)
    x_s = x_sample.reshape(n_s, D_MODEL)
    outs = {k: [] for k in ("nk_p", "nv_p", "nki_p", "nmk_p", "nmv_p", "ns_p", "nk_s", "nv_s", "nki_s", "ns_s")}

    for l in range(depth):
        a = _inproj(x_p, x_s, tabs, g_mix_norm[l], w_in[l], b_gate[l], w_gla_rank[l], b_gla_rank[l],
                    tiles_per_seq=tp // tm)

        n_qt = tp // tm
        oa_p = _dsa(a["qi"], a["wi"], a["qa"], a["kipad"], a["kpad"], a["vpad"], batch=bp, tq=tm, n_q_tiles=n_qt,
                    q_block0=0, k_rows=tp, topk=topk_p, qpos_fn=lambda j: j * tm,
                    nkb_fn=lambda j: (j + 1) * (tm // KEY_BLOCK), l_valid=tp, name="dsa_prompt")
        kipad_s, kpad_s, vpad_s = _key_side_sample(cache_k[l], cache_v[l], cache_kidx[l], a["kpad"][n_p:],
                                                   a["vpad"][n_p:], a["kipad"][n_p:], ks_rows)
        oa_s = _dsa(a["qi"], a["wi"], a["qa"], kipad_s, kpad_s, vpad_s, batch=bs, tq=ts, n_q_tiles=1,
                    q_block0=n_p // ts, k_rows=ks_rows, topk=topk_s, qpos_fn=lambda j: past,
                    nkb_fn=lambda j: ks_rows // KEY_BLOCK, l_valid=past + ts, name="dsa_sample")

        s_zero = jnp.zeros((bp, N_HEADS_G, DK_G, DV_G), F32)
        ob_p, s_p = _gla(a["qg"], a["kg"], a["lg"], a["vg"], a["og"], g_gla_norm[l], s_zero, batch=bp, tt=tm,
                         n_tiles=tp // tm, block0=0, chunk=CHUNK, name="gla_prompt")
        ob_s, s_s = _gla(a["qg"], a["kg"], a["lg"], a["vg"], a["og"], g_gla_norm[l], state_gla[l], batch=bs, tt=ts,
                         n_tiles=1, block0=n_p // ts, chunk=ts, name="gla_sample")

        mk, mv = _memkv(mem_prompt.reshape(bp * N_MEM, D_MODEL), g_mem_norm[l], w_mem_kv[l])
        om_p = _memattn(a["qm"], mk, mv, batch=bp, tq=tm, n_tiles=tp // tm, block0=0, name="mem_prompt")
        om_s = _memattn(a["qm"], cache_mem_k[l].reshape(bs * N_MEM, W_M), cache_mem_v[l].reshape(bs * N_MEM, W_M),
                        batch=bs, tq=ts, n_tiles=1, block0=n_p // ts, name="mem_sample")

        x2, hn, rgate, rcode, cnt = _merge((x_p, x_s), (oa_p, oa_s), (ob_p, ob_s), (om_p, om_s), a["gates"],
                                           w_br_a[l], w_br_b[l], w_br_m[l], w_o[l], g_ffn_norm[l], w_router[l],
                                           b_router[l], shift=shift)

        counts = cnt[0, :N_EXPERTS].astype(I32)
        nblk = (counts + EXPERT_ROWS - 1) // EXPERT_ROWS
        cum = jnp.cumsum(nblk)
        pstart = ((cum - nblk) * EXPERT_ROWS).astype(I32)
        n_used = cum[-1].reshape(1).astype(I32)
        blocks_before = jnp.sum((cum[None, :] <= jnp.arange(n_blocks, dtype=I32)[:, None]).astype(I32), axis=1)
        be = jnp.minimum(blocks_before, N_EXPERTS - 1)
        code = rcode[:, :TOP_K]

        xs = _dispatch(pstart, counts, n_used, code, hn, shift=shift, n_blocks=n_blocks)
        ys = _experts(be, n_used, xs, w_gu[l], b_gu[l], w_dn[l], b_dn[l], n_blocks=n_blocks)
        x_p, x_s = _combine(pstart, code, x2, rgate, g_final, ys, shift=shift, final_norm=(l == depth - 1),
                            n_prompt_rows=n_p)

        outs["nk_p"].append(a["ka"][:n_p].reshape(bp, tp, N_KV_A, HEAD_DIM_A))
        outs["nv_p"].append(a["va"][:n_p].reshape(bp, tp, N_KV_A, HEAD_DIM_A))
        outs["nki_p"].append(a["ki"][:n_p].reshape(bp, tp, IDX_DIM))
        outs["nmk_p"].append(mk.reshape(bp, N_MEM, N_HEADS_M, HEAD_DIM_M))
        outs["nmv_p"].append(mv.reshape(bp, N_MEM, N_HEADS_M, HEAD_DIM_M))
        outs["ns_p"].append(s_p)
        outs["nk_s"].append(a["ka"][n_p:].reshape(bs, ts, N_KV_A, HEAD_DIM_A))
        outs["nv_s"].append(a["va"][n_p:].reshape(bs, ts, N_KV_A, HEAD_DIM_A))
        outs["nki_s"].append(a["ki"][n_p:].reshape(bs, ts, IDX_DIM))
        outs["ns_s"].append(s_s)

    y_prompt = x_p.reshape(bp, tp, D_MODEL)
    y_sample = x_s.reshape(bs, ts, D_MODEL)
    st = lambda k: jnp.stack(outs[k])
    return (y_prompt, y_sample, st("nk_p"), st("nv_p"), st("nki_p"), st("nmk_p"), st("nmv_p"), st("ns_p"),
            st("nk_s"), st("nv_s"), st("nki_s"), st("ns_s"))
```

```python
import functools

import jax
import jax.numpy as jnp
from jax import lax
from jax.experimental import pallas as pl
from jax.experimental.pallas import tpu as pltpu

D_MODEL = 1024
CHUNK = 64
CHUNK_SHIFT = CHUNK.bit_length() - 1
ROPE_THETA = 10000.0
EPS = 1e-6
N_HEADS_A = 8
N_KV_A = 2
HEAD_DIM_A = 64
N_HEADS_IDX = 8
IDX_DIM = 64
TOPK_MAX = 256
N_HEADS_G = 4
DK_G = 64
DV_G = 128
GATE_RANK = 16
GATE_TAU = 16.0
N_MEM = 256
N_HEADS_M = 4
HEAD_DIM_M = 128
N_EXPERTS = 32
TOP_K = 4
SWIGLU_LIMIT = 7.0
SWIGLU_ALPHA = 1.702
N_BRANCH = 3

W_A = N_HEADS_A * HEAD_DIM_A
W_KVA = N_KV_A * HEAD_DIM_A
W_QI = N_HEADS_IDX * IDX_DIM
W_KG = N_HEADS_G * DK_G
W_VG = N_HEADS_G * DV_G
W_M = N_HEADS_M * HEAD_DIM_M
IN_WIDTHS = (W_A, W_KVA, W_KVA, W_QI, IDX_DIM, N_HEADS_IDX, W_KG, W_KG, W_VG, GATE_RANK, W_VG, W_M,
             N_BRANCH * D_MODEL)

LANES = 128
ROW_TILE = 256
KEY_BLOCK = 256
EXPERT_ROWS = 512
MEM_ROWS = 1024
GLA_ROWS = 512
VMEM_LIMIT = 56 * 1024 * 1024
INT_MIN = -2147483648

F32 = jnp.float32
BF16 = jnp.bfloat16
I32 = jnp.int32


def _cparams(sem):
    return pltpu.CompilerParams(dimension_semantics=sem, vmem_limit_bytes=VMEM_LIMIT)


def _dot(a, b):
    return jnp.dot(a, b, preferred_element_type=F32)


def _dot_nt(a, b):
    return lax.dot_general(a, b, (((1,), (1,)), ((), ())), preferred_element_type=F32)


def _dot_tn(a, b):
    return lax.dot_general(a, b, (((0,), (0,)), ((), ())), preferred_element_type=F32)


def _lane_iota(shape):
    return lax.broadcasted_iota(I32, shape, len(shape) - 1)


def _row_iota(shape):
    return lax.broadcasted_iota(I32, shape, len(shape) - 2)


def _rope_slab(z, cos_t, sin_s):
    w = z.shape[1]
    reps = w // LANES
    cos_w = jnp.concatenate([cos_t] * reps, axis=1) if reps > 1 else cos_t
    sin_w = jnp.concatenate([sin_s] * reps, axis=1) if reps > 1 else sin_s
    first = (_lane_iota(z.shape) & (HEAD_DIM_A - 1)) < (HEAD_DIM_A // 2)
    nxt = pltpu.roll(z, w - HEAD_DIM_A // 2, axis=1)
    prv = pltpu.roll(z, HEAD_DIM_A // 2, axis=1)
    return z * cos_w + jnp.where(first, nxt, prv) * sin_w


def _inproj_kernel(xp_ref, xs_ref, g_ref, cosp_ref, sinp_ref, coss_ref, sins_ref, wrope_ref, wplain_ref,
                   wsmall_ref, wgate_ref, bgate_ref, wrank_ref, brank_ref,
                   qa_ref, qi_ref, kpad_ref, kipad_ref, vpad_ref, wi_ref,
                   qg_ref, kg_ref, vg_ref, lg_ref, og_ref, qm_ref, gates_ref, vt_ref,
                   knew_p_ref, knew_s_ref, kinew_p_ref, kinew_s_ref, vnew_p_ref, vnew_s_ref, h_ref, *, n_prompt_tiles):
    is_p = pl.program_id(0) < n_prompt_tiles
    tm = xp_ref.shape[0]

    def emit(p_ref, s_ref, pieces):
        for idx, val in pieces:
            p_ref[idx] = jnp.where(is_p, val, p_ref[idx])
            s_ref[idx] = val

    def per_head_rows(x, x_sw):
        return [((pl.ds(0, tm, stride=N_KV_A), slice(None)), x[:, 0:HEAD_DIM_A]),
                ((pl.ds(1, tm, stride=N_KV_A), slice(None)), x_sw[:, 0:HEAD_DIM_A])]
    x = jnp.where(is_p, xp_ref[...], xs_ref[...])
    y = x * lax.rsqrt(jnp.mean(x * x, axis=-1, keepdims=True) + EPS)
    h_ref[...] = (y * g_ref[...]).astype(BF16)
    cos_t = jnp.where(is_p, cosp_ref[...], coss_ref[...])
    sin_s = jnp.where(is_p, sinp_ref[...], sins_ref[...])
    lo = _lane_iota((x.shape[0], LANES)) < HEAD_DIM_A

    c_qi, c_k = W_A, W_A + W_QI
    qa = _rope_slab(_dot(h_ref[...], wrope_ref[:, 0:c_qi]), cos_t, sin_s)
    qa_ref[...] = (qa * (HEAD_DIM_A ** -0.5)).astype(BF16)
    qi = _rope_slab(_dot(h_ref[...], wrope_ref[:, c_qi:c_k]), cos_t, sin_s)
    qi_ref[...] = qi.astype(BF16)
    kk = _rope_slab(_dot(h_ref[...], wrope_ref[:, c_k:c_k + W_KVA + LANES]), cos_t, sin_s)
    ka = kk[:, 0:LANES]
    ka_sw = pltpu.roll(ka, HEAD_DIM_A, axis=1)
    emit(knew_p_ref, knew_s_ref, per_head_rows(ka, ka_sw))
    zero = jnp.zeros_like(ka)
    kpad_ref[:, 0:128] = jnp.where(lo, ka, zero).astype(BF16)
    kpad_ref[:, 128:256] = jnp.where(lo, zero, ka_sw).astype(BF16)
    kpad_ref[:, 256:384] = jnp.where(lo, ka_sw, zero).astype(BF16)
    kpad_ref[:, 384:512] = jnp.where(lo, zero, ka).astype(BF16)
    kis = kk[:, LANES:2 * LANES]
    emit(kinew_p_ref, kinew_s_ref, [((slice(None), slice(None)), kis[:, 0:IDX_DIM])])
    kipad_ref[:, 0:128] = kis.astype(BF16)
    kipad_ref[:, 128:256] = pltpu.roll(kis, IDX_DIM, axis=1).astype(BF16)

    c_qg, c_kg, c_vg = W_KVA, W_KVA + W_KG, W_KVA + 2 * W_KG
    c_og, c_qm = c_vg + W_VG, c_vg + 2 * W_VG
    vq = _dot(h_ref[...], wplain_ref[:, 0:c_vg])
    va = vq[:, 0:c_qg]
    va_sw = pltpu.roll(va, HEAD_DIM_A, axis=1)
    emit(vnew_p_ref, vnew_s_ref, per_head_rows(va, va_sw))
    vpad_ref[:, 0:128] = va.astype(BF16)
    vpad_ref[:, 128:256] = va_sw.astype(BF16)
    vt_ref[0] = va.T.astype(BF16)
    qg_ref[...] = vq[:, c_qg:c_kg] * (DK_G ** -0.5)
    kg_ref[...] = vq[:, c_kg:c_vg]
    vg_ref[...] = _dot(h_ref[...], wplain_ref[:, c_vg:c_og])
    og_ref[...] = _dot(h_ref[...], wplain_ref[:, c_og:c_qm])
    qm_ref[...] = _dot(h_ref[...], wplain_ref[:, c_qm:c_qm + W_M]).astype(BF16)

    sm = _dot(h_ref[...], wsmall_ref[...])
    wi_ref[...] = sm[:, 0:128] * (N_HEADS_IDX ** -0.5 * IDX_DIM ** -0.5)
    pre = _dot(sm[:, 128:256].astype(BF16), wrank_ref[...]) + brank_ref[...]
    log_sig = jnp.minimum(pre, 0.0) - jnp.log(1.0 + jnp.exp(-jnp.abs(pre)))
    lg_ref[...] = log_sig / GATE_TAU

    for c in range(N_BRANCH * D_MODEL // 512):
        gt = _dot(h_ref[...], wgate_ref[:, c * 512:(c + 1) * 512]) + bgate_ref[:, c * 512:(c + 1) * 512]
        gates_ref[:, c * 512:(c + 1) * 512] = (1.0 / (1.0 + jnp.exp(-gt))).astype(BF16)


def _split_w_in(w_in):
    cuts, acc = [], 0
    for w in IN_WIDTHS:
        cuts.append((acc, acc + w))
        acc += w
    return [w_in[:, a:b] for a, b in cuts]


def _two_group_maps(n_prompt_tiles):
    pmap = lambda i, *_: (jnp.minimum(i, n_prompt_tiles - 1), 0)
    smap = lambda i, *_: (jnp.maximum(i - n_prompt_tiles, 0), 0)
    return pmap, smap


def _inproj(xp, xs, tabs, g_mix, w_in, b_gate, w_rank, b_rank, *, tiles_per_seq):
    n = xp.shape[0] + xs.shape[0]
    qa, ka, va, qi, ki, wi, qg, kg, vg, ag, og, qm, gt = _split_w_in(w_in)
    zc = lambda k: jnp.zeros((D_MODEL, k), F32)
    wrope = jnp.concatenate([qa, qi, ka, ki, zc(LANES - IDX_DIM)], axis=1).astype(BF16)
    wplain = jnp.concatenate([va, qg, kg, vg, og, qm], axis=1).astype(BF16)
    wsmall = jnp.concatenate([wi, zc(LANES - N_HEADS_IDX), ag, zc(LANES - GATE_RANK)], axis=1).astype(BF16)
    wgate = gt.astype(BF16)
    wrank = jnp.concatenate([w_rank, jnp.zeros((LANES - GATE_RANK, W_KG), F32)], axis=0).astype(BF16)

    tm = ROW_TILE
    row = lambda w: pl.BlockSpec((tm, w), lambda i: (i, 0))
    full = lambda a: pl.BlockSpec(a.shape, lambda i: (0, 0))
    outs = [("qa", 512, BF16), ("qi", 512, BF16), ("kpad", 512, BF16), ("kipad", 256, BF16), ("vpad", 256, BF16),
            ("wi", 128, F32), ("qg", 256, F32), ("kg", 256, F32), ("vg", 512, F32), ("lg", 256, F32),
            ("og", 512, F32), ("qm", 512, BF16), ("gates", N_BRANCH * D_MODEL, BF16)]
    g2 = g_mix.reshape(1, D_MODEL)
    bg2 = b_gate.reshape(1, -1)
    br2 = b_rank.reshape(1, -1)
    npt = xp.shape[0] // tm
    pmap, smap = _two_group_maps(npt)
    tab_p = pl.BlockSpec((tm, LANES), lambda i: (jnp.minimum(i, npt - 1) % tiles_per_seq, 0))
    tab_s = pl.BlockSpec((tm, LANES), lambda i: (0, 0))
    res = pl.pallas_call(
        functools.partial(_inproj_kernel, n_prompt_tiles=npt),
        grid=(n // tm,),
        in_specs=[pl.BlockSpec((tm, D_MODEL), pmap), pl.BlockSpec((tm, D_MODEL), smap), full(g2), tab_p, tab_p,
                  tab_s, tab_s, full(wrope), full(wplain), full(wsmall), full(wgate), full(bg2), full(wrank),
                  full(br2)],
        out_specs=[row(w) for _, w, _ in outs] + [pl.BlockSpec((1, W_KVA, tm), lambda i: (i, 0, 0))]
        + [pl.BlockSpec((rows * tm, HEAD_DIM_A), imap) for rows in (N_KV_A, 1, N_KV_A) for imap in (pmap, smap)],
        out_shape=[jax.ShapeDtypeStruct((n, w), dt) for _, w, dt in outs]
        + [jax.ShapeDtypeStruct((n // tm, W_KVA, tm), BF16)]
        + [jax.ShapeDtypeStruct((rows * r, HEAD_DIM_A), F32) for rows in (N_KV_A, 1, N_KV_A)
           for r in (xp.shape[0], xs.shape[0])],
        scratch_shapes=[pltpu.VMEM((tm, D_MODEL), BF16)],
        compiler_params=_cparams(("arbitrary",)),
        name="inproj",
    )(xp, xs, g2, *tabs, wrope, wplain, wsmall, wgate, bg2, wrank, br2)
    named = {name: r for (name, _, _), r in zip(outs, res)}
    for name, r in zip(("vt", "knew_p", "knew_s", "kinew_p", "kinew_s", "vnew_p", "vnew_s"), res[len(outs):]):
        named[name] = r
    return named


def _rope_tables(pos):
    half = HEAD_DIM_A // 2
    inv_freq = ROPE_THETA ** (-jnp.arange(half, dtype=F32) / half)
    ang = pos.astype(F32)[:, None] * inv_freq[None, :]
    cos = jnp.cos(ang)
    sin = jnp.sin(ang)
    cos_t = jnp.concatenate([cos, cos, cos, cos], axis=1)
    sin_s = jnp.concatenate([-sin, sin, -sin, sin], axis=1)
    return cos_t, sin_s


def _dsa_kernel(qi_ref, wi_ref, qa_ref, kipad_ref, kpad_ref, vpad_ref, o_ref,
                key_ref, bias_ref, l_ref, wb_ref, m_ref, s_ref, acc_ref, *, tq, nkb_max, topk, qpos_fn, nkb_fn,
                l_valid):
    j = pl.program_id(1)
    nkb = nkb_fn(j)
    qpos0 = qpos_fn(j)
    kb_w = KEY_BLOCK

    for hh in range(N_HEADS_IDX):
        wcol = jnp.broadcast_to(wi_ref[:, hh:hh + 1], (tq, LANES))
        wb_ref[hh] = jnp.concatenate([wcol, wcol], axis=1)

    qchunk = (qpos0 + _row_iota((tq, kb_w))) >> CHUNK_SHIFT

    def score_body(kb, carry):
        k0 = pl.multiple_of(kb * kb_w, kb_w)
        ka_blk = kipad_ref[pl.ds(k0, kb_w), 0:128]
        kb_blk = kipad_ref[pl.ds(k0, kb_w), 128:256]
        s = jnp.zeros((tq, kb_w), F32)
        for p in range(N_HEADS_IDX // 2):
            lhs = qi_ref[:, p * LANES:(p + 1) * LANES]
            s = s + jnp.maximum(_dot_nt(lhs, ka_blk), 0.0) * wb_ref[2 * p]
            s = s + jnp.maximum(_dot_nt(lhs, kb_blk), 0.0) * wb_ref[2 * p + 1]
        bits = lax.bitcast_convert_type(s, I32)
        key = bits ^ ((bits >> 31) & 0x7FFFFFFF)
        kidx = k0 + _lane_iota((tq, kb_w))
        adm = jnp.logical_and((kidx >> CHUNK_SHIFT) <= qchunk, kidx < l_valid)
        key_ref[kb] = jnp.where(adm, key, INT_MIN)
        return carry

    lax.fori_loop(0, nkb, score_body, 0)

    def count(pred_fn):
        def body(kb, acc):
            blk = key_ref[kb]
            kidx = kb * kb_w + _lane_iota((tq, kb_w))
            c = jnp.where(pred_fn(blk, kidx), 1.0, 0.0)
            return acc + c[:, 0:LANES] + c[:, LANES:2 * LANES]
        acc = lax.fori_loop(0, nkb, body, jnp.zeros((tq, LANES), F32))
        return jnp.sum(acc, axis=1, keepdims=True)

    def wide(v):
        return jnp.concatenate([v, v], axis=1)

    kf = float(topk)

    def search():
        c0 = count(lambda blk, kidx: blk >= 0)
        thr0 = jnp.where(jnp.broadcast_to(c0, (tq, LANES)) >= kf, 0, INT_MIN).astype(I32)

        def bit_body(i, thr):
            cand = thr + jnp.left_shift(jnp.int32(1), 30 - i)
            cand_w = wide(cand)
            c = count(lambda blk, kidx: blk >= cand_w)
            return jnp.where(jnp.broadcast_to(c, (tq, LANES)) >= kf, cand, thr)

        return lax.fori_loop(0, 31, bit_body, thr0)

    thr = lax.cond(nkb * kb_w > topk, search, lambda: jnp.full((tq, LANES), INT_MIN, I32))
    thr_w = wide(thr)

    n_gt = count(lambda blk, kidx: blk > thr_w)
    n_ge = count(lambda blk, kidx: blk >= thr_w)
    need = jnp.broadcast_to(kf - n_gt, (tq, LANES))
    idx_cap = nkb_max * kb_w

    def tie_search():
        nbits = max(1, (idx_cap - 1).bit_length())

        def bit_body(i, c):
            cand = c + jnp.left_shift(jnp.int32(1), nbits - 1 - i)
            cand_w = wide(cand)
            cnt = count(lambda blk, kidx: jnp.logical_and(blk == thr_w, kidx < cand_w))
            return jnp.where(jnp.broadcast_to(cnt, (tq, LANES)) <= need, cand, c)

        return lax.fori_loop(0, nbits, bit_body, jnp.zeros((tq, LANES), I32))

    has_ties = jnp.max(n_ge) > kf
    cut_w = wide(lax.cond(has_ties, tie_search, lambda: jnp.full((tq, LANES), idx_cap, I32)))

    def bias_body(kb, carry):
        blk = key_ref[kb]
        kidx = kb * kb_w + _lane_iota((tq, kb_w))
        sel = jnp.logical_or(blk > thr_w, jnp.logical_and(blk == thr_w, kidx < cut_w))
        sel = jnp.logical_and(sel, blk != INT_MIN)
        bias_ref[kb] = jnp.where(sel, 0.0, -jnp.inf)
        return carry

    lax.fori_loop(0, nkb, bias_body, 0)

    heads = []
    for p in range(N_HEADS_A // 2):
        n = (2 * p) // (N_HEADS_A // N_KV_A)
        for e in range(2):
            heads.append((p, (2 * n + e) * LANES, LANES if n != e else 0))

    m_ref[...] = jnp.full(m_ref.shape, -jnp.inf, F32)
    s_ref[...] = jnp.zeros(s_ref.shape, F32)
    acc_ref[...] = jnp.zeros(acc_ref.shape, F32)

    def logit_body(kb, carry):
        k0 = pl.multiple_of(kb * kb_w, kb_w)
        bias = bias_ref[kb]
        for hd, (p, kcol, _) in enumerate(heads):
            lg = _dot_nt(qa_ref[:, p * LANES:(p + 1) * LANES], kpad_ref[pl.ds(k0, kb_w), kcol:kcol + LANES]) + bias
            l_ref[hd, kb] = lg
            m_ref[hd] = jnp.maximum(m_ref[hd], jnp.maximum(lg[:, 0:LANES], lg[:, LANES:2 * LANES]))
        return carry

    lax.fori_loop(0, nkb, logit_body, 0)

    for hd in range(N_HEADS_A):
        m_ref[hd] = jnp.broadcast_to(jnp.max(m_ref[hd], axis=1, keepdims=True), (tq, LANES))

    def pv_body(kb, carry):
        k0 = pl.multiple_of(kb * kb_w, kb_w)
        for hd, (_, _, vcol) in enumerate(heads):
            ex = jnp.exp(l_ref[hd, kb] - wide(m_ref[hd]))
            s_ref[hd] = s_ref[hd] + ex[:, 0:LANES] + ex[:, LANES:2 * LANES]
            acc_ref[hd] = acc_ref[hd] + _dot(ex.astype(BF16), vpad_ref[pl.ds(k0, kb_w), vcol:vcol + LANES])
        return carry

    lax.fori_loop(0, nkb, pv_body, 0)

    lo = _lane_iota((tq, LANES)) < HEAD_DIM_A
    for p in range(N_HEADS_A // 2):
        halves = [acc_ref[2 * p + e] / jnp.sum(s_ref[2 * p + e], axis=1, keepdims=True) for e in range(2)]
        o_ref[:, p * LANES:(p + 1) * LANES] = jnp.where(lo, halves[0], halves[1]).astype(BF16)


def _dsa(qi, wi, qa, kipad, kpad, vpad, *, batch, tq, n_q_tiles, q_block0, k_rows, topk, qpos_fn, nkb_fn,
         l_valid, name):
    nkb_max = k_rows // KEY_BLOCK
    qmap = lambda b, j: (q_block0 + b * n_q_tiles + j, 0)
    kmap = lambda b, j: (b, 0)
    kern = functools.partial(_dsa_kernel, tq=tq, nkb_max=nkb_max, topk=topk, qpos_fn=qpos_fn, nkb_fn=nkb_fn,
                             l_valid=l_valid)
    return pl.pallas_call(
        kern,
        grid=(batch, n_q_tiles),
        in_specs=[pl.BlockSpec((tq, W_QI), qmap), pl.BlockSpec((tq, LANES), qmap), pl.BlockSpec((tq, W_A), qmap),
                  pl.BlockSpec((k_rows, 256), kmap), pl.BlockSpec((k_rows, 512), kmap),
                  pl.BlockSpec((k_rows, 256), kmap)],
        out_specs=pl.BlockSpec((tq, W_A), lambda b, j: (b * n_q_tiles + j, 0)),
        out_shape=jax.ShapeDtypeStruct((batch * n_q_tiles * tq, W_A), BF16),
        scratch_shapes=[pltpu.VMEM((nkb_max, tq, KEY_BLOCK), I32),
                        pltpu.VMEM((nkb_max, tq, KEY_BLOCK), F32),
                        pltpu.VMEM((N_HEADS_A, nkb_max, tq, KEY_BLOCK), F32),
                        pltpu.VMEM((N_HEADS_IDX, tq, KEY_BLOCK), F32),
                        pltpu.VMEM((N_HEADS_A, tq, LANES), F32),
                        pltpu.VMEM((N_HEADS_A, tq, LANES), F32),
                        pltpu.VMEM((N_HEADS_A, tq, LANES), F32)],
        compiler_params=_cparams(("parallel", "arbitrary")),
        name=name,
    )(qi, wi, qa, kipad, kpad, vpad)


def _dsa_t_kernel(qi_ref, wi_ref, qa_ref, kipad_ref, kpad_ref, vt_ref, o_ref,
                  key_ref, bias_ref, l_ref, m_ref, s_ref, acc_ref, *, tq, nkb_max, topk, qpos_fn, nkb_fn, l_valid):
    j = pl.program_id(1)
    nkb = nkb_fn(j)
    qpos0 = qpos_fn(j)
    kb_w = KEY_BLOCK
    sub = 8

    def fold(x, op):
        acc = x[0:sub, :]
        for g in range(1, kb_w // sub):
            acc = op(acc, x[g * sub:(g + 1) * sub, :])
        return acc

    wt = wi_ref[...].T
    qchunk = (qpos0 + _lane_iota((kb_w, tq))) >> CHUNK_SHIFT

    def score_body(kb, carry):
        k0 = pl.multiple_of(kb * kb_w, kb_w)
        ka_blk = kipad_ref[pl.ds(k0, kb_w), 0:128]
        kb_blk = kipad_ref[pl.ds(k0, kb_w), 128:256]
        s = jnp.zeros((kb_w, tq), F32)
        for p in range(N_HEADS_IDX // 2):
            rhs = qi_ref[:, p * LANES:(p + 1) * LANES]
            s = s + jnp.maximum(_dot_nt(ka_blk, rhs), 0.0) * wt[2 * p:2 * p + 1, :]
            s = s + jnp.maximum(_dot_nt(kb_blk, rhs), 0.0) * wt[2 * p + 1:2 * p + 2, :]
        bits = lax.bitcast_convert_type(s, I32)
        key = bits ^ ((bits >> 31) & 0x7FFFFFFF)
        kidx = k0 + _row_iota((kb_w, tq))
        adm = jnp.logical_and((kidx >> CHUNK_SHIFT) <= qchunk, kidx < l_valid)
        key_ref[kb] = jnp.where(adm, key, INT_MIN)
        return carry

    lax.fori_loop(0, nkb, score_body, 0)

    def count(pred_fn):
        def body(kb, acc):
            blk = key_ref[kb]
            kidx = kb * kb_w + _row_iota((kb_w, tq))
            return acc + fold(jnp.where(pred_fn(blk, kidx), 1.0, 0.0), jnp.add)
        acc = lax.fori_loop(0, nkb, body, jnp.zeros((sub, tq), F32))
        return jnp.sum(acc, axis=0, keepdims=True)

    kf = float(topk)

    def search():
        c0 = count(lambda blk, kidx: blk >= 0)
        ok0 = c0 >= kf
        start = (jnp.where(ok0, 0, INT_MIN).astype(I32), jnp.where(ok0, c0, jnp.asarray(nkb * kb_w).astype(F32)))

        def bit_body(i, carry):
            thr, n_ge = carry
            cand = thr + jnp.left_shift(jnp.int32(1), 30 - i)
            c = count(lambda blk, kidx: blk >= cand)
            ok = c >= kf
            return jnp.where(ok, cand, thr), jnp.where(ok, c, n_ge)

        return lax.fori_loop(0, 31, bit_body, start)

    thr, n_ge = lax.cond(nkb * kb_w > topk, search,
                         lambda: (jnp.full((1, tq), INT_MIN, I32), jnp.zeros((1, tq), F32)))

    idx_cap = nkb_max * kb_w

    def tie_search():
        need = kf - count(lambda blk, kidx: blk > thr)
        nbits = max(1, (idx_cap - 1).bit_length())

        def bit_body(i, c):
            cand = c + jnp.left_shift(jnp.int32(1), nbits - 1 - i)
            cnt = count(lambda blk, kidx: jnp.logical_and(blk == thr, kidx < cand))
            return jnp.where(cnt <= need, cand, c)

        return lax.fori_loop(0, nbits, bit_body, jnp.zeros((1, tq), I32))

    has_ties = jnp.max(n_ge) > kf

    def bias_with_ties():
        cut = tie_search()

        def bias_body(kb, carry):
            blk = key_ref[kb]
            kidx = kb * kb_w + _row_iota((kb_w, tq))
            sel = jnp.logical_or(blk > thr, jnp.logical_and(blk == thr, kidx < cut))
            sel = jnp.logical_and(sel, blk != INT_MIN)
            bias_ref[kb] = jnp.where(sel, 0.0, -jnp.inf)
            return carry

        lax.fori_loop(0, nkb, bias_body, 0)

    def bias_no_ties():
        floor = jnp.maximum(thr, INT_MIN + 1)

        def bias_body(kb, carry):
            bias_ref[kb] = jnp.where(key_ref[kb] >= floor, 0.0, -jnp.inf)
            return carry

        lax.fori_loop(0, nkb, bias_body, 0)

    lax.cond(has_ties, bias_with_ties, bias_no_ties)

    heads = []
    for p in range(N_HEADS_A // 2):
        n = (2 * p) // (N_HEADS_A // N_KV_A)
        for e in range(2):
            heads.append((p, (2 * n + e) * LANES, n))

    m_ref[...] = jnp.full(m_ref.shape, -jnp.inf, F32)
    s_ref[...] = jnp.zeros(s_ref.shape, F32)
    acc_ref[...] = jnp.zeros(acc_ref.shape, F32)

    def logit_body(kb, carry):
        k0 = pl.multiple_of(kb * kb_w, kb_w)
        bias = bias_ref[kb]
        for hd, (p, kcol, _) in enumerate(heads):
            lg = _dot_nt(kpad_ref[pl.ds(k0, kb_w), kcol:kcol + LANES], qa_ref[:, p * LANES:(p + 1) * LANES]) + bias
            l_ref[hd, kb] = lg
            m_ref[hd] = jnp.maximum(m_ref[hd], fold(lg, jnp.maximum))
        return carry

    lax.fori_loop(0, nkb, logit_body, 0)

    for hd in range(N_HEADS_A):
        m_ref[hd] = jnp.broadcast_to(jnp.max(m_ref[hd], axis=0, keepdims=True), (sub, tq))

    def pv_body(kb, carry):
        for hd, (_, _, n) in enumerate(heads):
            ex = jnp.exp(l_ref[hd, kb] - m_ref[hd][0:1, :])
            s_ref[hd] = s_ref[hd] + fold(ex, jnp.add)
            vt = vt_ref[kb, n * HEAD_DIM_A:(n + 1) * HEAD_DIM_A, :]
            acc_ref[hd] = acc_ref[hd] + _dot(vt, ex.astype(BF16))
        return carry

    lax.fori_loop(0, nkb, pv_body, 0)

    o_t = jnp.concatenate([acc_ref[hd] / jnp.sum(s_ref[hd], axis=0, keepdims=True) for hd in range(N_HEADS_A)],
                          axis=0)
    o_ref[...] = o_t.T.astype(BF16)


def _dsa_t(qi, wi, qa, kipad, kpad, vt, *, batch, tq, n_q_tiles, q_block0, k_rows, topk, qpos_fn, nkb_fn, l_valid,
           name):
    nkb_max = k_rows // KEY_BLOCK
    qmap = lambda b, j: (q_block0 + b * n_q_tiles + j, 0)
    kmap = lambda b, j: (b, 0)
    kern = functools.partial(_dsa_t_kernel, tq=tq, nkb_max=nkb_max, topk=topk, qpos_fn=qpos_fn, nkb_fn=nkb_fn,
                             l_valid=l_valid)
    return pl.pallas_call(
        kern,
        grid=(batch, n_q_tiles),
        in_specs=[pl.BlockSpec((tq, W_QI), qmap), pl.BlockSpec((tq, LANES), qmap), pl.BlockSpec((tq, W_A), qmap),
                  pl.BlockSpec((k_rows, 256), kmap), pl.BlockSpec((k_rows, 512), kmap),
                  pl.BlockSpec((nkb_max, W_KVA, KEY_BLOCK), lambda b, j: (b, 0, 0))],
        out_specs=pl.BlockSpec((tq, W_A), lambda b, j: (b * n_q_tiles + j, 0)),
        out_shape=jax.ShapeDtypeStruct((batch * n_q_tiles * tq, W_A), BF16),
        scratch_shapes=[pltpu.VMEM((nkb_max, KEY_BLOCK, tq), I32),
                        pltpu.VMEM((nkb_max, KEY_BLOCK, tq), F32),
                        pltpu.VMEM((N_HEADS_A, nkb_max, KEY_BLOCK, tq), F32),
                        pltpu.VMEM((N_HEADS_A, 8, tq), F32),
                        pltpu.VMEM((N_HEADS_A, 8, tq), F32),
                        pltpu.VMEM((N_HEADS_A, HEAD_DIM_A, tq), F32)],
        compiler_params=_cparams(("parallel", "arbitrary")),
        name=name,
    )(qi, wi, qa, kipad, kpad, vt)


def _cumsum_rows(x):
    c = x.shape[0]
    rows = _row_iota(x.shape)
    s = 1
    while s < c:
        x = x + jnp.where(rows >= s, pltpu.roll(x, s, axis=0), 0.0)
        s *= 2
    return x


def _gla_kernel(q_ref, k_ref, lg_ref, v_ref, og_ref, gn_ref, s0_ref, o_ref, sout_ref, st_ref, *, tt, chunk):
    t = pl.program_id(1)
    n_pairs = N_HEADS_G // 2

    @pl.when(t == 0)
    def _():
        for p in range(n_pairs):
            pair = jnp.concatenate([s0_ref[2 * p], s0_ref[2 * p + 1]], axis=0)
            st_ref[p] = pair.T

    lane = _lane_iota((chunk, LANES))
    masks = (lane < DK_G, lane >= DK_G)
    causal = _row_iota((chunk, chunk)) >= _lane_iota((chunk, chunk))

    for c in range(tt // chunk):
        r0 = c * chunk
        bcum = _cumsum_rows(lg_ref[r0:r0 + chunk, :])
        for p in range(n_pairs):
            sl = slice(p * LANES, (p + 1) * LANES)
            b = bcum[:, sl]
            q = q_ref[r0:r0 + chunk, sl]
            k = k_ref[r0:r0 + chunk, sl]
            b_last = b[chunk - 1:chunk, :]
            b_mid = b[chunk // 2:chunk // 2 + 1, :]
            qb = q * jnp.exp(b)
            qe = (q * jnp.exp(b - b_mid)).astype(BF16)
            ke = k * jnp.exp(b_mid - b)
            kd = k * jnp.exp(b_last - b)
            st_old = st_ref[p]
            st_bf = st_old.astype(BF16)
            upd = jnp.zeros((LANES, LANES), F32)
            for e in range(2):
                hd = 2 * p + e
                vh = v_ref[r0:r0 + chunk, hd * DV_G:(hd + 1) * DV_G]
                a = _dot_nt(qe, jnp.where(masks[e], ke, 0.0).astype(BF16))
                a = jnp.where(causal, a, 0.0)
                o = _dot(a.astype(BF16), vh.astype(BF16))
                o = o + _dot_nt(jnp.where(masks[e], qb, 0.0).astype(BF16), st_bf)
                kd_m = jnp.where(masks[e], kd, 0.0)
                upd = upd + _dot_tn(vh.astype(BF16), kd_m.astype(BF16))
                on = o * lax.rsqrt(jnp.mean(o * o, axis=-1, keepdims=True) + EPS)
                on = on * gn_ref[:, hd * DV_G:(hd + 1) * DV_G]
                og = og_ref[r0:r0 + chunk, hd * DV_G:(hd + 1) * DV_G]
                o_ref[r0:r0 + chunk, hd * DV_G:(hd + 1) * DV_G] = (on * (og / (1.0 + jnp.exp(-og)))).astype(BF16)
            st_ref[p] = st_old * jnp.exp(b_last) + upd

    @pl.when(t == pl.num_programs(1) - 1)
    def _():
        for p in range(n_pairs):
            pair = st_ref[p].T
            sout_ref[2 * p] = pair[0:DK_G, :]
            sout_ref[2 * p + 1] = pair[DK_G:2 * DK_G, :]


def _gla(qg, kg, lg, vg, og, g_norm, s0, *, batch, tt, n_tiles, block0, chunk, name):
    rmap = lambda b, t: (block0 + b * n_tiles + t, 0)
    smap = lambda b, t: (b, 0, 0, 0)
    gn2 = g_norm.reshape(1, W_VG)
    kern = functools.partial(_gla_kernel, tt=tt, chunk=chunk)
    return pl.pallas_call(
        kern,
        grid=(batch, n_tiles),
        in_specs=[pl.BlockSpec((tt, W_KG), rmap), pl.BlockSpec((tt, W_KG), rmap), pl.BlockSpec((tt, W_KG), rmap),
                  pl.BlockSpec((tt, W_VG), rmap), pl.BlockSpec((tt, W_VG), rmap),
                  pl.BlockSpec((1, W_VG), lambda b, t: (0, 0)),
                  pl.BlockSpec((None, N_HEADS_G, DK_G, DV_G), smap)],
        out_specs=[pl.BlockSpec((tt, W_VG), lambda b, t: (b * n_tiles + t, 0)),
                   pl.BlockSpec((None, N_HEADS_G, DK_G, DV_G), smap)],
        out_shape=[jax.ShapeDtypeStruct((batch * n_tiles * tt, W_VG), BF16),
                   jax.ShapeDtypeStruct((batch, N_HEADS_G, DK_G, DV_G), F32)],
        scratch_shapes=[pltpu.VMEM((N_HEADS_G // 2, LANES, LANES), F32)],
        compiler_params=_cparams(("parallel", "arbitrary")),
        name=name,
    )(qg, kg, lg, vg, og, gn2, s0)


def _memkv_kernel(m_ref, g_ref, w_ref, mk_ref, mv_ref):
    x = m_ref[...]
    y = x * lax.rsqrt(jnp.mean(x * x, axis=-1, keepdims=True) + EPS)
    h = (y * g_ref[...]).astype(BF16)
    mk_ref[...] = _dot(h, w_ref[:, 0:W_M])
    mv_ref[...] = _dot(h, w_ref[:, W_M:2 * W_M])


def _memkv(mem_rows, g_mem, w_mem_kv):
    n = mem_rows.shape[0]
    tm = ROW_TILE
    g2 = g_mem.reshape(1, D_MODEL)
    wb = w_mem_kv.astype(BF16)
    return pl.pallas_call(
        _memkv_kernel,
        grid=(n // tm,),
        in_specs=[pl.BlockSpec((tm, D_MODEL), lambda i: (i, 0)), pl.BlockSpec((1, D_MODEL), lambda i: (0, 0)),
                  pl.BlockSpec(wb.shape, lambda i: (0, 0))],
        out_specs=[pl.BlockSpec((tm, W_M), lambda i: (i, 0))] * 2,
        out_shape=[jax.ShapeDtypeStruct((n, W_M), F32)] * 2,
        compiler_params=_cparams(("parallel",)),
        name="memkv",
    )(mem_rows, g2, wb)


def _memattn_kernel(q_ref, mk_ref, mv_ref, o_ref):
    rows = q_ref.shape[0]
    chunk = min(rows, ROW_TILE)
    for hd in range(N_HEADS_M):
        sl = slice(hd * HEAD_DIM_M, (hd + 1) * HEAD_DIM_M)
        mk = mk_ref[:, sl].astype(BF16)
        mv = mv_ref[:, sl].astype(BF16)
        for c in range(rows // chunk):
            rs = slice(c * chunk, (c + 1) * chunk)
            logits = _dot_nt(q_ref[rs, sl], mk) * (HEAD_DIM_M ** -0.5)
            m = jnp.max(logits, axis=-1, keepdims=True)
            ex = jnp.exp(logits - m)
            o = _dot(ex.astype(BF16), mv)
            o_ref[rs, sl] = (o / jnp.sum(ex, axis=-1, keepdims=True)).astype(BF16)


def _memattn(qm, mk, mv, *, batch, tq, n_tiles, block0, name):
    qmap = lambda b, t: (block0 + b * n_tiles + t, 0)
    kmap = lambda b, t: (b, 0)
    return pl.pallas_call(
        _memattn_kernel,
        grid=(batch, n_tiles),
        in_specs=[pl.BlockSpec((tq, W_M), qmap), pl.BlockSpec((N_MEM, W_M), kmap),
                  pl.BlockSpec((N_MEM, W_M), kmap)],
        out_specs=pl.BlockSpec((tq, W_M), lambda b, t: (b * n_tiles + t, 0)),
        out_shape=jax.ShapeDtypeStruct((batch * n_tiles * tq, W_M), BF16),
        compiler_params=_cparams(("parallel", "arbitrary")),
        name=name,
    )(qm, mk, mv)


def _merge_kernel(xp_ref, xs_ref, oap_ref, oas_ref, obp_ref, obs_ref, omp_ref, oms_ref, gates_ref, wa_ref, wb_ref,
                  wm_ref, wo_ref, gf_ref, wr_ref, br_ref, x2_ref, hn_ref, rgate_ref, rpos_ref, tcnt_ref, toff_ref,
                  tbase_ref, cnt_ref, run_ref, *, n_prompt_tiles):
    i = pl.program_id(0)
    tm = xp_ref.shape[0]
    is_p = i < n_prompt_tiles
    pick = lambda p_ref, s_ref: jnp.where(is_p, p_ref[...], s_ref[...])

    @pl.when(i == 0)
    def _():
        run_ref[...] = jnp.zeros_like(run_ref)

    d = D_MODEL
    m = gates_ref[:, 0:d].astype(F32) * _dot(pick(oap_ref, oas_ref), wa_ref[...])
    m = m + gates_ref[:, d:2 * d].astype(F32) * _dot(pick(obp_ref, obs_ref), wb_ref[...])
    m = m + gates_ref[:, 2 * d:3 * d].astype(F32) * _dot(pick(omp_ref, oms_ref), wm_ref[...])
    x2 = pick(xp_ref, xs_ref) + _dot(m.astype(BF16), wo_ref[...])
    x2_ref[...] = x2
    hn = x2 * lax.rsqrt(jnp.mean(x2 * x2, axis=-1, keepdims=True) + EPS) * gf_ref[...]
    hn = hn.astype(BF16)
    hn_ref[...] = hn

    lane = _lane_iota((tm, LANES))
    lane_f = lane.astype(F32)
    logits = _dot(hn, wr_ref[...]) + br_ref[...]
    work = jnp.where(lane < N_EXPERTS, logits, -jnp.inf)
    vals, hots, eidx = [], [], []
    for _ in range(TOP_K):
        mx = jnp.max(work, axis=-1, keepdims=True)
        idx = jnp.min(jnp.where(work == mx, lane_f, float(LANES)), axis=-1, keepdims=True)
        hot = lane_f == idx
        vals.append(mx)
        hots.append(hot)
        eidx.append(idx)
        work = jnp.where(hot, -jnp.inf, work)
    exps = [jnp.exp(v - vals[0]) for v in vals]
    den = exps[0] + exps[1] + exps[2] + exps[3]

    multi = jnp.where(jnp.logical_or(jnp.logical_or(hots[0], hots[1]), jnp.logical_or(hots[2], hots[3])), 1.0, 0.0)
    tri = jnp.where(_row_iota((tm, tm)) > _lane_iota((tm, tm)), 1.0, 0.0).astype(BF16)
    before = _dot(tri, multi.astype(BF16))
    tile_cnt = jnp.sum(multi, axis=0, keepdims=True)
    upper = jnp.where(_row_iota((LANES, LANES)) < _lane_iota((LANES, LANES)), 1.0, 0.0).astype(BF16)
    tile_off = _dot(jnp.broadcast_to(tile_cnt, (8, LANES)).astype(BF16), upper)[0:1, :]
    tcnt_ref[0] = tile_cnt.astype(I32)
    toff_ref[0] = tile_off.astype(I32)
    tbase_ref[0] = run_ref[...].astype(I32)
    run_ref[...] = run_ref[...] + tile_cnt
    cnt_ref[...] = run_ref[...]

    where_in_tile = before + tile_off
    rgate = jnp.zeros((tm, LANES), F32)
    rpos = jnp.zeros((tm, LANES), F32)
    for jj in range(TOP_K):
        pos = jnp.sum(jnp.where(hots[jj], where_in_tile, 0.0), axis=-1, keepdims=True)
        sel = lane == jj
        rgate = jnp.where(sel, exps[jj] / den, rgate)
        rpos = jnp.where(sel, pos, rpos)
    rgate_ref[...] = rgate
    rpos_ref[...] = rpos.astype(I32)


def _merge(x, oa, ob, om, gates, w_br_a, w_br_b, w_br_m, w_o, g_ffn, w_router, b_router):
    n = gates.shape[0]
    tm = ROW_TILE
    npt = x[0].shape[0] // tm
    pmap, smap = _two_group_maps(npt)
    pair = lambda w: [pl.BlockSpec((tm, w), pmap), pl.BlockSpec((tm, w), smap)]
    row = lambda w: pl.BlockSpec((tm, w), lambda i: (i, 0))
    full = lambda a: pl.BlockSpec(a.shape, lambda i: (0, 0))
    wa, wb, wm, wo = (w.astype(BF16) for w in (w_br_a, w_br_b, w_br_m, w_o))
    gf2 = g_ffn.reshape(1, D_MODEL)
    wr = jnp.concatenate([w_router, jnp.zeros((D_MODEL, LANES - N_EXPERTS), F32)], axis=1).astype(BF16)
    br = jnp.concatenate([b_router, jnp.zeros((LANES - N_EXPERTS,), F32)]).reshape(1, LANES)
    kern = functools.partial(_merge_kernel, n_prompt_tiles=npt)
    tile_row = pl.BlockSpec((1, 1, LANES), lambda i: (i, 0, 0))
    tile_shape = jax.ShapeDtypeStruct((n // tm, 1, LANES), I32)
    return pl.pallas_call(
        kern,
        grid=(n // tm,),
        in_specs=[*pair(D_MODEL), *pair(W_A), *pair(W_VG), *pair(W_M), row(N_BRANCH * D_MODEL), full(wa), full(wb),
                  full(wm), full(wo), full(gf2), full(wr), full(br)],
        out_specs=[row(D_MODEL), row(D_MODEL), row(LANES), row(LANES), tile_row, tile_row, tile_row,
                   pl.BlockSpec((1, LANES), lambda i: (0, 0))],
        out_shape=[jax.ShapeDtypeStruct((n, D_MODEL), F32), jax.ShapeDtypeStruct((n, D_MODEL), BF16),
                   jax.ShapeDtypeStruct((n, LANES), F32), jax.ShapeDtypeStruct((n, LANES), I32),
                   tile_shape, tile_shape, tile_shape, jax.ShapeDtypeStruct((1, LANES), F32)],
        scratch_shapes=[pltpu.VMEM((1, LANES), F32)],
        compiler_params=_cparams(("arbitrary",)),
        name="merge",
    )(*x, *oa, *ob, *om, gates, wa, wb, wm, wo, gf2, wr, br)


ROW_SUB = D_MODEL // LANES
TOKEN_UNROLL = 16


def _to_row_tiled(ref, x):
    for c in range(ROW_SUB):
        ref[pl.ds(c, x.shape[0], stride=ROW_SUB), :] = x[:, c * LANES:(c + 1) * LANES]


def _from_row_tiled(ref, rows):
    return jnp.concatenate([ref[pl.ds(c, rows, stride=ROW_SUB), :] for c in range(ROW_SUB)], axis=1)


def _tiled_rows(ref, start, size):
    return ref.at[pl.ds(pl.multiple_of(start * ROW_SUB, ROW_SUB), size * ROW_SUB), :]


def _for_each_piece(length, pos_a, pos_b, max_piece, fn):
    size = max_piece
    while size >= 1:
        take = length & size

        @pl.when(take != 0)
        def _(pos_a=pos_a, pos_b=pos_b, size=size):
            fn(pos_a, pos_b, size)

        pos_a = pos_a + take
        pos_b = pos_b + take
        size //= 2


def _for_each_run_piece(tcnt_ref, toff_ref, tbase_ref, pstart_ref, fn):
    def expert_body(e, carry):
        _for_each_piece(tcnt_ref[0, 0, e], toff_ref[0, 0, e], pstart_ref[e] + tbase_ref[0, 0, e], ROW_TILE, fn)
        return carry

    lax.fori_loop(0, N_EXPERTS, expert_body, 0)


def _dispatch_kernel(pstart_ref, cnt_ref, nu_ref, pos_ref, tcnt_ref, toff_ref, tbase_ref, hn_ref, xs_ref,
                     x3_ref, stage_ref, zero_ref, sem, zsem, *, n_blocks):
    i = pl.program_id(0)
    last = pl.num_programs(0) - 1
    tm = hn_ref.shape[0]
    er = EXPERT_ROWS
    buf = i & 1
    stage = stage_ref.at[buf]

    _to_row_tiled(x3_ref, hn_ref[...].astype(F32))

    def place(g, carry):
        for u in range(TOKEN_UNROLL):
            t = g * TOKEN_UNROLL + u
            row = x3_ref[pl.ds(pl.multiple_of(t * ROW_SUB, ROW_SUB), ROW_SUB), :]
            for jj in range(TOP_K):
                q = pos_ref[0, 0, t * TOP_K + jj]
                stage[pl.ds(pl.multiple_of(q * ROW_SUB, ROW_SUB), ROW_SUB), :] = row
        return carry

    lax.fori_loop(0, tm // TOKEN_UNROLL, place, 0)

    def run_copy(b):
        def make(tile_pos, slot, size):
            return pltpu.make_async_copy(_tiled_rows(stage_ref.at[b], tile_pos, size), _tiled_rows(xs_ref, slot, size),
                                         sem.at[b])
        return make

    _for_each_run_piece(tcnt_ref, toff_ref, tbase_ref, pstart_ref, lambda a, b, z: run_copy(buf)(a, b, z).start())

    @pl.when(i == last)
    def _():
        zero_ref[...] = jnp.zeros_like(zero_ref)

        def zero_copy(_, slot, size):
            cp = pltpu.make_async_copy(_tiled_rows(zero_ref, 0, size), _tiled_rows(xs_ref, slot, size), zsem)
            cp.start()
            cp.wait()

        def expert_body(e, carry):
            c = cnt_ref[e]
            _for_each_piece((er - c % er) % er, 0, pstart_ref[e] + c, er // 2, zero_copy)
            return carry

        lax.fori_loop(0, N_EXPERTS, expert_body, 0)

        def block_body(b, carry):
            zero_copy(0, b * er, er)
            return carry

        lax.fori_loop(nu_ref[0], n_blocks, block_body, 0)

    def wait_runs(b):
        pltpu.make_async_copy(stage_ref.at[b], _tiled_rows(xs_ref, 0, tm * TOP_K), sem.at[b]).wait()

    @pl.when(i > 0)
    def _():
        wait_runs(1 - buf)

    @pl.when(i == last)
    def _():
        wait_runs(buf)


def _dispatch(pstart, counts, nused, pos, tcnt, toff, tbase, hn, *, n_blocks):
    n = hn.shape[0]
    tm = ROW_TILE
    pos3 = pos.reshape(n // tm, 1, tm * TOP_K)
    smem = lambda w, imap: pl.BlockSpec((1, 1, w), imap, memory_space=pltpu.SMEM)
    cur = lambda i, *_: (i, 0, 0)
    kern = functools.partial(_dispatch_kernel, n_blocks=n_blocks)
    return pl.pallas_call(
        kern,
        grid_spec=pltpu.PrefetchScalarGridSpec(
            num_scalar_prefetch=3,
            grid=(n // tm,),
            in_specs=[smem(tm * TOP_K, cur), smem(LANES, cur), smem(LANES, cur), smem(LANES, cur),
                      pl.BlockSpec((tm, D_MODEL), lambda i, *_: (i, 0))],
            out_specs=pl.BlockSpec(memory_space=pl.ANY),
            scratch_shapes=[pltpu.VMEM((tm * ROW_SUB, LANES), F32),
                            pltpu.VMEM((2, tm * TOP_K * ROW_SUB, LANES), F32),
                            pltpu.VMEM((EXPERT_ROWS * ROW_SUB, LANES), F32),
                            pltpu.SemaphoreType.DMA((2,)), pltpu.SemaphoreType.DMA],
        ),
        out_shape=jax.ShapeDtypeStruct((n_blocks * EXPERT_ROWS * ROW_SUB, LANES), F32),
        compiler_params=_cparams(("arbitrary",)),
        name="dispatch",
    )(pstart, counts, nused, pos3, tcnt, toff, tbase, hn)


def _experts_kernel(be_ref, nu_ref, x_ref, wgu_ref, bgu_ref, wdn_ref, bdn_ref, y_ref, wgu_bf, wdn_bf):
    i = pl.program_id(0)
    cast_rows = 128

    @pl.when(jnp.logical_or(i == 0, be_ref[i] != be_ref[jnp.maximum(i - 1, 0)]))
    def _():
        def cast_body(c, carry):
            r0 = pl.multiple_of(c * cast_rows, cast_rows)
            wgu_bf[pl.ds(r0, cast_rows), :] = wgu_ref[pl.ds(r0, cast_rows), :].astype(BF16)
            wdn_bf[pl.ds(r0, cast_rows), :] = wdn_ref[pl.ds(r0, cast_rows), :].astype(BF16)
            return carry

        lax.fori_loop(0, D_MODEL // cast_rows, cast_body, 0)

    @pl.when(i < nu_ref[0])
    def _():
        xb = _from_row_tiled(x_ref, EXPERT_ROWS).astype(BF16)
        de = D_MODEL
        g = _dot(xb, wgu_bf[:, 0:de]) + bgu_ref[:, 0:de]
        u = _dot(xb, wgu_bf[:, de:2 * de]) + bgu_ref[:, de:2 * de]
        g = jnp.minimum(g, SWIGLU_LIMIT)
        u = jnp.clip(u, -SWIGLU_LIMIT, SWIGLU_LIMIT)
        a = (u + 1.0) * (g * (1.0 / (1.0 + jnp.exp(-SWIGLU_ALPHA * g))))
        _to_row_tiled(y_ref, _dot(a.astype(BF16), wdn_bf[...]) + bdn_ref[...])

    @pl.when(i >= nu_ref[0])
    def _():
        y_ref[...] = jnp.zeros_like(y_ref)


def _experts(be, nused, xs, wgu, b_gu, wdn, b_dn, *, n_blocks):
    bgu = b_gu.reshape(N_EXPERTS, 1, 2 * D_MODEL)
    bdn = b_dn.reshape(N_EXPERTS, 1, D_MODEL)
    blk = pl.BlockSpec((EXPERT_ROWS * ROW_SUB, LANES), lambda i, be, nu: (i, 0))
    return pl.pallas_call(
        _experts_kernel,
        grid_spec=pltpu.PrefetchScalarGridSpec(
            num_scalar_prefetch=2,
            grid=(n_blocks,),
            in_specs=[blk,
                      pl.BlockSpec((None, D_MODEL, 2 * D_MODEL), lambda i, be, nu: (be[i], 0, 0)),
                      pl.BlockSpec((None, 1, 2 * D_MODEL), lambda i, be, nu: (be[i], 0, 0)),
                      pl.BlockSpec((None, D_MODEL, D_MODEL), lambda i, be, nu: (be[i], 0, 0)),
                      pl.BlockSpec((None, 1, D_MODEL), lambda i, be, nu: (be[i], 0, 0))],
            out_specs=blk,
            scratch_shapes=[pltpu.VMEM((D_MODEL, 2 * D_MODEL), BF16), pltpu.VMEM((D_MODEL, D_MODEL), BF16)],
        ),
        out_shape=jax.ShapeDtypeStruct(xs.shape, F32),
        compiler_params=_cparams(("arbitrary",)),
        name="experts",
    )(be, nused, xs, wgu, bgu, wdn, bdn)


def _combine_kernel(pstart_ref, pos_ref, gate_ref, tcnt_ref, toff_ref, tbase_ref, ncnt_ref, noff_ref, nbase_ref,
                    x2_ref, gfin_ref, y_ref, outp_ref, outs_ref, stage_ref, moe_ref, sem, *, final_norm,
                    n_prompt_tiles):
    i = pl.program_id(0)
    tm = x2_ref.shape[0]
    buf = i & 1
    stage = stage_ref.at[buf]

    def run_copy(b):
        def make(tile_pos, slot, size):
            return pltpu.make_async_copy(_tiled_rows(y_ref, slot, size), _tiled_rows(stage_ref.at[b], tile_pos, size),
                                         sem.at[b])
        return make

    @pl.when(i == 0)
    def _():
        _for_each_run_piece(tcnt_ref, toff_ref, tbase_ref, pstart_ref, lambda a, b, z: run_copy(buf)(a, b, z).start())

    @pl.when(i < pl.num_programs(0) - 1)
    def _():
        _for_each_run_piece(ncnt_ref, noff_ref, nbase_ref, pstart_ref,
                            lambda a, b, z: run_copy(1 - buf)(a, b, z).start())

    pltpu.make_async_copy(_tiled_rows(y_ref, 0, tm * TOP_K), stage_ref.at[buf], sem.at[buf]).wait()

    def mix(g, carry):
        for u in range(TOKEN_UNROLL):
            t = g * TOKEN_UNROLL + u
            acc = jnp.zeros((ROW_SUB, LANES), F32)
            for jj in range(TOP_K):
                q = pos_ref[0, 0, t * TOP_K + jj]
                row = stage[pl.ds(pl.multiple_of(q * ROW_SUB, ROW_SUB), ROW_SUB), :]
                acc = acc + row * gate_ref[0, 0, t * TOP_K + jj]
            moe_ref[pl.ds(pl.multiple_of(t * ROW_SUB, ROW_SUB), ROW_SUB), :] = acc
        return carry

    lax.fori_loop(0, tm // TOKEN_UNROLL, mix, 0)

    acc = x2_ref[...] + _from_row_tiled(moe_ref, tm)
    if final_norm:
        acc = acc * lax.rsqrt(jnp.mean(acc * acc, axis=-1, keepdims=True) + EPS) * gfin_ref[...]

    is_p = pl.program_id(0) < n_prompt_tiles
    outp_ref[...] = jnp.where(is_p, acc, outp_ref[...])
    outs_ref[...] = acc


def _combine(pstart, pos, gate, tcnt, toff, tbase, x2, g_final, ys, *, final_norm, n_prompt_rows):
    n = x2.shape[0]
    tm = ROW_TILE
    npt = n_prompt_rows // tm
    pmap, smap = _two_group_maps(npt)
    pos3 = pos.reshape(n // tm, 1, tm * TOP_K)
    gate3 = gate.reshape(n // tm, 1, tm * TOP_K)
    gf2 = g_final.reshape(1, D_MODEL)
    n_tiles = n // tm
    smem = lambda w, imap: pl.BlockSpec((1, 1, w), imap, memory_space=pltpu.SMEM)
    cur = lambda i, ps: (i, 0, 0)
    nxt = lambda i, ps: (jnp.minimum(i + 1, n_tiles - 1), 0, 0)
    return pl.pallas_call(
        functools.partial(_combine_kernel, final_norm=final_norm, n_prompt_tiles=npt),
        grid_spec=pltpu.PrefetchScalarGridSpec(
            num_scalar_prefetch=1,
            grid=(n_tiles,),
            in_specs=[smem(tm * TOP_K, cur), smem(tm * TOP_K, cur), smem(LANES, cur), smem(LANES, cur),
                      smem(LANES, cur), smem(LANES, nxt), smem(LANES, nxt), smem(LANES, nxt),
                      pl.BlockSpec((tm, D_MODEL), lambda i, ps: (i, 0)),
                      pl.BlockSpec((1, D_MODEL), lambda i, ps: (0, 0)),
                      pl.BlockSpec(memory_space=pl.ANY)],
            out_specs=[pl.BlockSpec((tm, D_MODEL), pmap), pl.BlockSpec((tm, D_MODEL), smap)],
            scratch_shapes=[pltpu.VMEM((2, tm * TOP_K * ROW_SUB, LANES), F32),
                            pltpu.VMEM((tm * ROW_SUB, LANES), F32), pltpu.SemaphoreType.DMA((2,))],
        ),
        out_shape=[jax.ShapeDtypeStruct((n_prompt_rows, D_MODEL), F32),
                   jax.ShapeDtypeStruct((n - n_prompt_rows, D_MODEL), F32)],
        compiler_params=_cparams(("arbitrary",)),
        name="combine",
    )(pstart, pos3, gate3, tcnt, toff, tbase, tcnt, toff, tbase, x2, gf2, ys)


def _key_side_sample(cache_k, cache_v, cache_kidx, kpad_new, vpad_new, kipad_new, k_rows):
    db, past = cache_k.shape[0], cache_k.shape[1]
    ck = cache_k.reshape(db, past, W_KVA)
    cv = cache_v.reshape(db, past, W_KVA)
    z64 = jnp.zeros((db, past, HEAD_DIM_A), F32)
    k0, k1 = ck[..., :HEAD_DIM_A], ck[..., HEAD_DIM_A:]
    kpad_c = jnp.concatenate([k0, z64, z64, k0, k1, z64, z64, k1], axis=-1).astype(BF16)
    vpad_c = jnp.concatenate([cv, cv[..., HEAD_DIM_A:], cv[..., :HEAD_DIM_A]], axis=-1).astype(BF16)
    kipad_c = jnp.concatenate([cache_kidx, z64, z64, cache_kidx], axis=-1).astype(BF16)
    ds = kpad_new.shape[0] // db
    pad = k_rows - past - ds

    def cat(c, new):
        w = c.shape[-1]
        parts = [c, new.reshape(db, ds, w)]
        if pad:
            parts.append(jnp.zeros((db, pad, w), BF16))
        return jnp.concatenate(parts, axis=1).reshape(db * k_rows, w)

    return cat(kipad_c, kipad_new), cat(kpad_c, kpad_new), cat(vpad_c, vpad_new)


def kernel(x_prompt, x_sample, mem_prompt, cache_k, cache_v, cache_kidx, cache_mem_k, cache_mem_v, state_gla,
           g_mix_norm, w_in, b_gate, w_gla_rank, b_gla_rank, g_gla_norm, g_mem_norm, w_mem_kv, w_br_a, w_br_b,
           w_br_m, w_o, g_ffn_norm, w_router, b_router, w_gu, b_gu, w_dn, b_dn, g_final):
    bp, tp, _ = x_prompt.shape
    bs, ts, _ = x_sample.shape
    depth = w_in.shape[0]
    past = cache_k.shape[2]
    n_p, n_s = bp * tp, bs * ts
    n = n_p + n_s
    tm = ROW_TILE
    assert n_s % tm == 0 and tp % tm == 0 and n_p % ts == 0 and tp % KEY_BLOCK == 0 and tm % ts == 0
    assert ts <= CHUNK and past % CHUNK == 0 and ts % 16 == 0
    topk_p = min(TOPK_MAX, tp // 4)
    topk_s = min(TOPK_MAX, (past + ts) // 4)
    ks_rows = -(-(past + ts) // KEY_BLOCK) * KEY_BLOCK
    n_blocks = -(-(n * TOP_K) // EXPERT_ROWS) + N_EXPERTS

    tabs = (*_rope_tables(jnp.arange(tp, dtype=I32)),
            *_rope_tables(jnp.tile(past + jnp.arange(ts, dtype=I32), tm // ts)))

    x_p = x_prompt.reshape(n_p, D_MODEL)
    x_s = x_sample.reshape(n_s, D_MODEL)
    outs = {k: [] for k in ("nk_p", "nv_p", "nki_p", "nmk_p", "nmv_p", "ns_p", "nk_s", "nv_s", "nki_s", "ns_s")}

    for l in range(depth):
        a = _inproj(x_p, x_s, tabs, g_mix_norm[l], w_in[l], b_gate[l], w_gla_rank[l], b_gla_rank[l],
                    tiles_per_seq=tp // tm)

        n_qt = tp // tm
        oa_p = _dsa_t(a["qi"], a["wi"], a["qa"], a["kipad"], a["kpad"], a["vt"], batch=bp, tq=tm, n_q_tiles=n_qt,
                      q_block0=0, k_rows=tp, topk=topk_p, qpos_fn=lambda j: j * tm,
                      nkb_fn=lambda j: (j + 1) * (tm // KEY_BLOCK), l_valid=tp, name="dsa_prompt")
        kipad_s, kpad_s, vpad_s = _key_side_sample(cache_k[l], cache_v[l], cache_kidx[l], a["kpad"][n_p:],
                                                   a["vpad"][n_p:], a["kipad"][n_p:], ks_rows)
        oa_s = _dsa(a["qi"], a["wi"], a["qa"], kipad_s, kpad_s, vpad_s, batch=bs, tq=ts, n_q_tiles=1,
                    q_block0=n_p // ts, k_rows=ks_rows, topk=topk_s, qpos_fn=lambda j: past,
                    nkb_fn=lambda j: ks_rows // KEY_BLOCK, l_valid=past + ts, name="dsa_sample")

        s_zero = jnp.zeros((bp, N_HEADS_G, DK_G, DV_G), F32)
        tt_g = GLA_ROWS if tp % GLA_ROWS == 0 else tm
        ob_p, s_p = _gla(a["qg"], a["kg"], a["lg"], a["vg"], a["og"], g_gla_norm[l], s_zero, batch=bp, tt=tt_g,
                         n_tiles=tp // tt_g, block0=0, chunk=CHUNK, name="gla_prompt")
        ob_s, s_s = _gla(a["qg"], a["kg"], a["lg"], a["vg"], a["og"], g_gla_norm[l], state_gla[l], batch=bs, tt=ts,
                         n_tiles=1, block0=n_p // ts, chunk=ts, name="gla_sample")

        mk, mv = _memkv(mem_prompt.reshape(bp * N_MEM, D_MODEL), g_mem_norm[l], w_mem_kv[l])
        tq_m = MEM_ROWS if tp % MEM_ROWS == 0 else tm
        om_p = _memattn(a["qm"], mk, mv, batch=bp, tq=tq_m, n_tiles=tp // tq_m, block0=0, name="mem_prompt")
        om_s = _memattn(a["qm"], cache_mem_k[l].reshape(bs * N_MEM, W_M), cache_mem_v[l].reshape(bs * N_MEM, W_M),
                        batch=bs, tq=ts, n_tiles=1, block0=n_p // ts, name="mem_sample")

        x2, hn, rgate, rpos, tcnt, toff, tbase, cnt = _merge(
            (x_p, x_s), (oa_p, oa_s), (ob_p, ob_s), (om_p, om_s), a["gates"], w_br_a[l], w_br_b[l], w_br_m[l],
            w_o[l], g_ffn_norm[l], w_router[l], b_router[l])

        counts = cnt[0, :N_EXPERTS].astype(I32)
        nblk = (counts + EXPERT_ROWS - 1) // EXPERT_ROWS
        cum = jnp.cumsum(nblk)
        pstart = ((cum - nblk) * EXPERT_ROWS).astype(I32)
        n_used = cum[-1].reshape(1).astype(I32)
        blocks_before = jnp.sum((cum[None, :] <= jnp.arange(n_blocks, dtype=I32)[:, None]).astype(I32), axis=1)
        be = jnp.minimum(blocks_before, N_EXPERTS - 1)
        pos = rpos[:, :TOP_K]

        xs = _dispatch(pstart, counts, n_used, pos, tcnt, toff, tbase, hn, n_blocks=n_blocks)
        ys = _experts(be, n_used, xs, w_gu[l], b_gu[l], w_dn[l], b_dn[l], n_blocks=n_blocks)
        x_p, x_s = _combine(pstart, pos, rgate[:, :TOP_K], tcnt, toff, tbase, x2, g_final, ys,
                            final_norm=(l == depth - 1), n_prompt_rows=n_p)

        outs["nk_p"].append(a["knew_p"].reshape(bp, tp, N_KV_A, HEAD_DIM_A))
        outs["nv_p"].append(a["vnew_p"].reshape(bp, tp, N_KV_A, HEAD_DIM_A))
        outs["nki_p"].append(a["kinew_p"].reshape(bp, tp, IDX_DIM))
        outs["nmk_p"].append(mk.reshape(bp, N_MEM, N_HEADS_M, HEAD_DIM_M))
        outs["nmv_p"].append(mv.reshape(bp, N_MEM, N_HEADS_M, HEAD_DIM_M))
        outs["ns_p"].append(s_p)
        outs["nk_s"].append(a["knew_s"].reshape(bs, ts, N_KV_A, HEAD_DIM_A))
        outs["nv_s"].append(a["vnew_s"].reshape(bs, ts, N_KV_A, HEAD_DIM_A))
        outs["nki_s"].append(a["kinew_s"].reshape(bs, ts, IDX_DIM))
        outs["ns_s"].append(s_s)

    y_prompt = x_p.reshape(bp, tp, D_MODEL)
    y_sample = x_s.reshape(bs, ts, D_MODEL)
    st = lambda k: jnp.stack(outs[k])
    return (y_prompt, y_sample, st("nk_p"), st("nv_p"), st("nki_p"), st("nmk_p"), st("nmv_p"), st("ns_p"),
            st("nk_s"), st("nv_s"), st("nki_s"), st("ns_s"))
```
